```python
import jax, jax.numpy as jnp
from jax import lax
import numpy as np

D_MODEL = 1024
BATCH = 4
SEQ = 8192
DEPTH = 1

N_Q_HEADS = 8
N_KV_HEADS = 2
HEAD_DIM = D_MODEL // 16
ATTN_WIDTH = N_Q_HEADS * HEAD_DIM
KV_WIDTH = N_KV_HEADS * HEAD_DIM
WINDOW = 128
BLOCK = 128
POOL_WIDTH = D_MODEL // 2
POOL_WINDOWS = (2, 4, 8, 16)
N_POOL_GROUPS = 4
POOL_GROUP = POOL_WIDTH // N_POOL_GROUPS
IN_WIDTH = ATTN_WIDTH + 2 * KV_WIDTH + ATTN_WIDTH + 2 * POOL_WIDTH + 2 * D_MODEL
EPS = 1e-6
NEG_INF = -1e30

kernel_name = "hybrid_gated_swa_pool_encoder"


def rms_norm(x, g):
    xf = x.astype(jnp.float32)
    xf = xf * lax.rsqrt(jnp.mean(xf * xf, axis=-1, keepdims=True) + EPS)
    return (xf * g.astype(jnp.float32)).astype(x.dtype)


def windowed_gqa(q, k, v, sink):
    B, S, _ = q.shape
    nb = S // BLOCK
    G = N_Q_HEADS // N_KV_HEADS
    qb = q.reshape(B, nb, BLOCK, N_KV_HEADS, G, HEAD_DIM)
    pad = ((0, 0), (BLOCK, BLOCK), (0, 0))
    kp = jnp.pad(k, pad).reshape(B, nb + 2, BLOCK, N_KV_HEADS, HEAD_DIM)
    vp = jnp.pad(v, pad).reshape(B, nb + 2, BLOCK, N_KV_HEADS, HEAD_DIM)
    kw = jnp.concatenate([kp[:, :-2], kp[:, 1:-1], kp[:, 2:]], axis=2)
    vw = jnp.concatenate([vp[:, :-2], vp[:, 1:-1], vp[:, 2:]], axis=2)

    scale = HEAD_DIM ** -0.5
    scores = jnp.einsum('bnqhgd,bnkhd->bnhgqk', qb, kw,
                        preferred_element_type=jnp.float32) * scale

    r = jnp.arange(BLOCK)
    j = jnp.arange(3 * BLOCK)
    rel = r[:, None] - (j[None, :] - BLOCK)
    dist = jnp.abs(rel).astype(jnp.float32)
    kpos = jnp.arange(nb)[:, None] * BLOCK - BLOCK + j[None, :]
    valid = (kpos >= 0) & (kpos < S)
    mask = (jnp.abs(rel) <= WINDOW)[None] & valid[:, None, :]

    slopes = jnp.exp2(-8.0 * jnp.arange(1, N_Q_HEADS + 1, dtype=jnp.float32) / N_Q_HEADS)
    slopes = slopes.reshape(N_KV_HEADS, G)
    alibi = -slopes[:, :, None, None] * dist[None, None]
    scores = jnp.where(mask[None, :, None, None], scores + alibi[None, None], NEG_INF)

    sink_f = sink.astype(jnp.float32).reshape(N_KV_HEADS, G)[None, None, :, :, None]
    m = jnp.maximum(jnp.max(scores, axis=-1), sink_f)
    p = jnp.exp(scores - m[..., None])
    denom = jnp.sum(p, axis=-1) + jnp.exp(sink_f - m)
    out = jnp.einsum('bnhgqk,bnkhd->bnqhgd', p, vw.astype(jnp.float32))
    out = out / jnp.transpose(denom, (0, 1, 4, 2, 3))[..., None]
    return out.reshape(B, S, ATTN_WIDTH).astype(q.dtype)


def multiscale_pool(u, pool_w, pool_scale):
    B, S, _ = u.shape
    uf = u.astype(jnp.float32).reshape(B, S, N_POOL_GROUPS, POOL_GROUP)
    csum = jnp.pad(jnp.cumsum(uf, axis=1), ((0, 0), (1, 0), (0, 0), (0, 0)))
    t = jnp.arange(S)
    outs = []
    for gi, w in enumerate(POOL_WINDOWS):
        lo = jnp.clip(t - w // 2, 0, S)
        hi = jnp.clip(t + w // 2, 0, S)
        window_sum = csum[:, hi, gi] - csum[:, lo, gi]
        outs.append(window_sum / (hi - lo).astype(jnp.float32)[None, :, None])
    pooled = jnp.stack(outs, axis=2) - uf
    mixed = jnp.einsum('bsgc,gcd->bsgd', pooled.astype(u.dtype), pool_w)
    mixed = mixed * pool_scale.reshape(N_POOL_GROUPS, POOL_GROUP)
    return mixed.reshape(B, S, POOL_WIDTH)


def setup_inputs(seed: int = 0) -> dict:
    key = jax.random.key(seed)
    ks = jax.random.split(key, 11)
    f32 = jnp.float32
    x = jax.random.normal(ks[0], (BATCH, SEQ, D_MODEL), f32)
    norm_g = 1.0 + 0.1 * jax.random.normal(ks[1], (DEPTH, D_MODEL), f32)
    w_in = jax.random.normal(ks[2], (DEPTH, D_MODEL, IN_WIDTH), f32) * D_MODEL ** -0.5
    attn_sink = jax.random.normal(ks[3], (DEPTH, N_Q_HEADS), f32)
    pool_w = jax.random.normal(ks[4], (DEPTH, N_POOL_GROUPS, POOL_GROUP, POOL_GROUP), f32) * POOL_GROUP ** -0.5
    pool_scale = 1.0 + 0.1 * jax.random.normal(ks[5], (DEPTH, POOL_WIDTH), f32)
    w_branch_a = jax.random.normal(ks[6], (DEPTH, ATTN_WIDTH, D_MODEL), f32) * ATTN_WIDTH ** -0.5
    w_branch_b = jax.random.normal(ks[7], (DEPTH, POOL_WIDTH, D_MODEL), f32) * POOL_WIDTH ** -0.5
    w_out = jax.random.normal(ks[8], (DEPTH, D_MODEL, D_MODEL), f32) * D_MODEL ** -0.5
    final_norm_g = 1.0 + 0.1 * jax.random.normal(ks[9], (D_MODEL,), f32)
    return {"x": x, "norm_g": norm_g, "w_in": w_in, "attn_sink": attn_sink,
            "pool_w": pool_w, "pool_scale": pool_scale, "w_branch_a": w_branch_a,
            "w_branch_b": w_branch_b, "w_out": w_out, "final_norm_g": final_norm_g}


def reference(x, norm_g, w_in, attn_sink, pool_w, pool_scale, w_branch_a, w_branch_b, w_out, final_norm_g):
    widths = (ATTN_WIDTH, KV_WIDTH, KV_WIDTH, ATTN_WIDTH, POOL_WIDTH, POOL_WIDTH, D_MODEL)
    offsets = []
    acc = 0
    for wd in widths:
        acc += wd
        offsets.append(acc)
    for l in range(DEPTH):
        h = rms_norm(x, norm_g[l])
        proj = jnp.einsum('bsd,de->bse', h, w_in[l])
        q, k, v, z_a, u_b, z_b, g_a, g_b = jnp.split(proj, offsets, axis=-1)
        attn = windowed_gqa(q, k, v, attn_sink[l]) * jax.nn.silu(z_a)
        y_a = jnp.einsum('bsc,cd->bsd', attn, w_branch_a[l])
        pool = multiscale_pool(u_b, pool_w[l], pool_scale[l]) * jax.nn.silu(z_b)
        y_b = jnp.einsum('bsc,cd->bsd', pool, w_branch_b[l])
        merged = jax.nn.sigmoid(g_a) * y_a + jax.nn.sigmoid(g_b) * y_b
        x = x + jnp.einsum('bsd,de->bse', merged, w_out[l])
    return rms_norm(x, final_norm_g)
```

```python
import functools

import numpy as np
import jax
import jax.numpy as jnp
from jax import lax
from jax.experimental import pallas as pl
from jax.experimental.pallas import tpu as pltpu

D_MODEL = 1024
N_Q_HEADS = 8
N_KV_HEADS = 2
GROUP = N_Q_HEADS // N_KV_HEADS
HEAD_DIM = 64
ATTN_WIDTH = N_Q_HEADS * HEAD_DIM
KV_WIDTH = N_KV_HEADS * HEAD_DIM
WINDOW = 128
BLOCK = 128
POOL_WIDTH = 512
POOL_WINDOWS = (2, 4, 8, 16)
POOL_GROUP = 128
EPS = 1e-6
NEG_INF = -1e30

OFF_Q = 0
OFF_K = OFF_Q + ATTN_WIDTH
OFF_V = OFF_K + KV_WIDTH
OFF_ZA = OFF_V + KV_WIDTH
OFF_U = OFF_ZA + ATTN_WIDTH
OFF_ZB = OFF_U + POOL_WIDTH
OFF_GA = OFF_ZB + POOL_WIDTH
OFF_GB = OFF_GA + D_MODEL
IN_WIDTH = OFF_GB + D_MODEL

TILE = 512
HALO = BLOCK
VMEM_LIMIT_BYTES = 56 * 1024 * 1024


def _attention_bias():
    r = np.arange(BLOCK)[:, None]
    c = np.arange(3 * BLOCK)[None, :]
    dist = np.abs(r - (c - BLOCK))
    in_window = dist <= WINDOW
    slopes = np.exp2(-8.0 * np.arange(1, N_Q_HEADS + 1, dtype=np.float64) / N_Q_HEADS)
    alibi = -slopes[:, None, None] * dist[None].astype(np.float64)
    valid = [np.ones_like(c, bool), c >= BLOCK, c < 2 * BLOCK]
    out = np.stack([np.where(in_window & v, alibi, NEG_INF) for v in valid])
    return out.reshape(3, N_KV_HEADS, GROUP * BLOCK, 3 * BLOCK).astype(np.float32)


def _rms_norm(x, g):
    ms = jnp.mean(x * x, axis=-1, keepdims=True)
    return x * lax.rsqrt(ms + EPS) * g


def _silu(x):
    return x * jax.nn.sigmoid(x)


def _block_kernel(xc_ref, xp_ref, xn_ref, norm_g_ref, w_in_ref, sink_ref, bias_ref,
                  pool_w_ref, pool_scale_ref, w_a_ref, w_b_ref, w_out_ref, final_g_ref,
                  out_ref, u_scr, attn_scr, *, seq_len):
    f32, bf16 = jnp.float32, jnp.bfloat16
    i = pl.program_id(1)
    n_tiles = pl.num_programs(1)
    n_blk = TILE // BLOCK

    def proj(h, off, width):
        return jnp.dot(h, w_in_ref[:, off:off + width], preferred_element_type=f32)

    norm_g = norm_g_ref[...]
    xc = xc_ref[...]
    hc = _rms_norm(xc, norm_g).astype(bf16)
    hp = jnp.where(i > 0, _rms_norm(xp_ref[...], norm_g), 0.0).astype(bf16)
    hn = jnp.where(i < n_tiles - 1, _rms_norm(xn_ref[...], norm_g), 0.0).astype(bf16)
    h_all = jnp.concatenate([hp, hc, hn], axis=0)

    k_all = proj(h_all, OFF_K, KV_WIDTH)
    v_all = proj(h_all, OFF_V, KV_WIDTH)
    lane = lax.broadcasted_iota(jnp.int32, (1, 2 * HEAD_DIM), 1)
    low = lane < HEAD_DIM

    def dup_heads(a):
        swapped = pltpu.roll(a, HEAD_DIM, axis=1)
        return (jnp.where(low, a, swapped).astype(bf16), jnp.where(low, swapped, a).astype(bf16))

    k_dup = dup_heads(k_all)
    v_dup = dup_heads(v_all)

    q = proj(hc, OFF_Q, ATTN_WIDTH) * (HEAD_DIM ** -0.5)
    zero = jnp.zeros((), f32)
    for hk in range(N_KV_HEADS):
        sink_col = sink_ref[hk]
        for j in range(n_blk):
            rows = slice(j * BLOCK, (j + 1) * BLOCK)
            pieces = []
            for pair in range(GROUP // 2):
                col = (hk * GROUP // 2 + pair) * 2 * HEAD_DIM
                qp = q[rows, col:col + 2 * HEAD_DIM]
                pieces.append(jnp.where(low, qp, zero))
                pieces.append(jnp.where(low, zero, qp))
            lhs = jnp.concatenate(pieces, axis=0).astype(bf16)
            kwin = k_dup[hk][j * BLOCK:j * BLOCK + 3 * BLOCK]
            vwin = v_dup[hk][j * BLOCK:j * BLOCK + 3 * BLOCK]
            s = lax.dot_general(lhs, kwin, (((1,), (1,)), ((), ())), preferred_element_type=f32)
            if j == 0:
                variant = jnp.where(i == 0, 1, 0)
            elif j == n_blk - 1:
                variant = jnp.where(i == n_tiles - 1, 2, 0)
            else:
                variant = 0
            s = s + bias_ref[variant, hk]
            m = jnp.maximum(jnp.max(s, axis=-1, keepdims=True), sink_col)
            p = jnp.exp(s - m)
            denom = jnp.sum(p, axis=-1, keepdims=True) + jnp.exp(sink_col - m)
            o = jnp.dot(p.astype(bf16), vwin, preferred_element_type=f32)
            o = o / denom
            for pair in range(GROUP // 2):
                even = o[(2 * pair) * BLOCK:(2 * pair + 1) * BLOCK]
                odd = o[(2 * pair + 1) * BLOCK:(2 * pair + 2) * BLOCK]
                col = (hk * GROUP // 2 + pair) * 2 * HEAD_DIM
                attn_scr[rows, col:col + 2 * HEAD_DIM] = jnp.where(low, even, odd)

    z_a = proj(hc, OFF_ZA, ATTN_WIDTH)
    gated_a = (attn_scr[...] * _silu(z_a)).astype(bf16)
    y_a = jnp.dot(gated_a, w_a_ref[...], preferred_element_type=f32)

    u_scr[...] = proj(h_all, OFF_U, POOL_WIDTH)
    pos = i * TILE + lax.broadcasted_iota(jnp.int32, (TILE, 1), 0)
    mixed = []
    for gi, w in enumerate(POOL_WINDOWS):
        cols = slice(gi * POOL_GROUP, (gi + 1) * POOL_GROUP)
        half = w // 2
        wsum = u_scr[HALO - half:HALO - half + TILE, cols]
        for d in range(-half + 1, half):
            wsum = wsum + u_scr[HALO + d:HALO + d + TILE, cols]
        count = (jnp.minimum(pos + half, seq_len) - jnp.maximum(pos - half, 0)).astype(f32)
        pooled = wsum / count - u_scr[HALO:HALO + TILE, cols]
        mixed.append(jnp.dot(pooled.astype(bf16), pool_w_ref[gi], preferred_element_type=f32))
    mixed = jnp.concatenate(mixed, axis=1) * pool_scale_ref[...]
    z_b = proj(hc, OFF_ZB, POOL_WIDTH)
    gated_b = (mixed * _silu(z_b)).astype(bf16)
    y_b = jnp.dot(gated_b, w_b_ref[...], preferred_element_type=f32)

    g_a = proj(hc, OFF_GA, D_MODEL)
    g_b = proj(hc, OFF_GB, D_MODEL)
    merged = (jax.nn.sigmoid(g_a) * y_a + jax.nn.sigmoid(g_b) * y_b).astype(bf16)
    y = xc + jnp.dot(merged, w_out_ref[...], preferred_element_type=f32)
    out_ref[...] = _rms_norm(y, final_g_ref[...])


def kernel(x, norm_g, w_in, attn_sink, pool_w, pool_scale, w_branch_a, w_branch_b, w_out, final_norm_g):
    batch, seq_len, d_model = x.shape
    depth = norm_g.shape[0]
    assert depth == 1, "the final RMSNorm is fused into the single layer's kernel"
    assert d_model == D_MODEL and w_in.shape[-1] == IN_WIDTH
    assert seq_len % TILE == 0 and seq_len // BLOCK >= 2
    bf16 = jnp.bfloat16
    bias = jnp.asarray(_attention_bias())
    n_tiles = seq_len // TILE
    blk_per_tile = TILE // BLOCK
    last_blk = seq_len // BLOCK - 1

    def resident(shape):
        return pl.BlockSpec(shape, lambda b, i: (0,) * len(shape), pipeline_mode=pl.Buffered(1))

    call = pl.pallas_call(
        functools.partial(_block_kernel, seq_len=seq_len),
        grid=(batch, n_tiles),
        in_specs=[
            pl.BlockSpec((None, TILE, D_MODEL), lambda b, i: (b, i, 0)),
            pl.BlockSpec((None, HALO, D_MODEL),
                         lambda b, i: (b, jnp.maximum(i * blk_per_tile - 1, 0), 0)),
            pl.BlockSpec((None, HALO, D_MODEL),
                         lambda b, i: (b, jnp.minimum((i + 1) * blk_per_tile, last_blk), 0)),
            resident((1, D_MODEL)),
            resident((D_MODEL, IN_WIDTH)),
            resident((N_KV_HEADS, GROUP * BLOCK, 1)),
            resident((3, N_KV_HEADS, GROUP * BLOCK, 3 * BLOCK)),
            resident((len(POOL_WINDOWS), POOL_GROUP, POOL_GROUP)),
            resident((1, POOL_WIDTH)),
            resident((ATTN_WIDTH, D_MODEL)),
            resident((POOL_WIDTH, D_MODEL)),
            resident((D_MODEL, D_MODEL)),
            resident((1, D_MODEL)),
        ],
        out_specs=pl.BlockSpec((None, TILE, D_MODEL), lambda b, i: (b, i, 0)),
        out_shape=jax.ShapeDtypeStruct(x.shape, x.dtype),
        scratch_shapes=[
            pltpu.VMEM((TILE + 2 * HALO, POOL_WIDTH), jnp.float32),
            pltpu.VMEM((TILE, ATTN_WIDTH), jnp.float32),
        ],
        compiler_params=pltpu.CompilerParams(
            dimension_semantics=("arbitrary", "arbitrary"),
            vmem_limit_bytes=VMEM_LIMIT_BYTES),
        name="hybrid_block",
    )

    for l in range(depth):
        sink_col = jnp.repeat(attn_sink[l].astype(jnp.float32), BLOCK).reshape(N_KV_HEADS, GROUP * BLOCK, 1)
        x = call(x, x, x,
                 norm_g[l].reshape(1, D_MODEL), w_in[l].astype(bf16), sink_col, bias,
                 pool_w[l].astype(bf16), pool_scale[l].reshape(1, POOL_WIDTH),
                 w_branch_a[l].astype(bf16), w_branch_b[l].astype(bf16), w_out[l].astype(bf16),
                 final_norm_g.reshape(1, D_MODEL))
    return x
```

```python
import functools

import numpy as np
import jax
import jax.numpy as jnp
from jax import lax
from jax.experimental import pallas as pl
from jax.experimental.pallas import tpu as pltpu

D_MODEL = 1024
N_Q_HEADS = 8
N_KV_HEADS = 2
GROUP = N_Q_HEADS // N_KV_HEADS
HEAD_DIM = 64
ATTN_WIDTH = N_Q_HEADS * HEAD_DIM
KV_WIDTH = N_KV_HEADS * HEAD_DIM
WINDOW = 128
BLOCK = 128
POOL_WIDTH = 512
POOL_WINDOWS = (2, 4, 8, 16)
POOL_GROUP = 128
EPS = 1e-6
NEG_INF = -1e30

OFF_Q = 0
OFF_K = OFF_Q + ATTN_WIDTH
OFF_V = OFF_K + KV_WIDTH
OFF_ZA = OFF_V + KV_WIDTH
OFF_U = OFF_ZA + ATTN_WIDTH
OFF_ZB = OFF_U + POOL_WIDTH
OFF_GA = OFF_ZB + POOL_WIDTH
OFF_GB = OFF_GA + D_MODEL
IN_WIDTH = OFF_GB + D_MODEL

TILE = 512
HALO = BLOCK
MXU_COLS = 256
TAIL_ROWS = 128
POOL_EDGE_ROWS = 16
POOL_LEAD = 16
POOL_TAIL = 32
VMEM_LIMIT_BYTES = 56 * 1024 * 1024


def _attention_bias():
    r = np.arange(BLOCK)[:, None]
    c = np.arange(3 * BLOCK)[None, :]
    dist = np.abs(r - (c - BLOCK))
    in_window = dist <= WINDOW
    slopes = np.exp2(-8.0 * np.arange(1, N_Q_HEADS + 1, dtype=np.float64) / N_Q_HEADS)
    alibi = -slopes[:, None, None] * dist[None].astype(np.float64)
    valid = [np.ones_like(c, bool), c >= BLOCK, c < 2 * BLOCK]
    out = np.stack([np.where(in_window & v, alibi, NEG_INF) for v in valid])
    return out.reshape(3, N_KV_HEADS, GROUP * BLOCK, 3 * BLOCK).astype(np.float32)


def _pool_inv_count(seq_len):
    out = np.zeros((2, 2, len(POOL_WINDOWS), POOL_EDGE_ROWS, 1), np.float64)
    for gi, w in enumerate(POOL_WINDOWS):
        half = w // 2
        out[:, 0, gi] = 1.0 / w
        for end, pos in enumerate([np.arange(POOL_EDGE_ROWS), seq_len - POOL_EDGE_ROWS + np.arange(POOL_EDGE_ROWS)]):
            count = np.minimum(pos + half, seq_len) - np.maximum(pos - half, 0)
            out[end, 1, gi, :, 0] = 1.0 / count
    return np.broadcast_to(out, out.shape[:-1] + (POOL_GROUP,)).astype(np.float32)


def _rms_norm(x, g):
    ms = jnp.mean(x * x, axis=-1, keepdims=True)
    return x * lax.rsqrt(ms + EPS) * g


def _silu(x):
    return x * jax.nn.sigmoid(x)


def _block_kernel(xc_ref, xp_ref, xn_ref, norm_g_ref, w_in_ref, sink_ref, bias_ref, cnt_ref,
                  pool_w_ref, pool_scale_ref, w_a_ref, w_b_ref, w_out_ref, final_g_ref,
                  out_ref, attn_scr):
    f32, bf16 = jnp.float32, jnp.bfloat16
    i = pl.program_id(1)
    n_tiles = pl.num_programs(1)
    n_blk = TILE // BLOCK
    is_first = i == 0
    is_last = i == n_tiles - 1

    def proj(h, off, width):
        return jnp.dot(h, w_in_ref[:, off:off + width], preferred_element_type=f32)

    def proj_chunks(h, off, width):
        return [functools.partial(proj, h, off + c, MXU_COLS) for c in range(0, width, MXU_COLS)]

    norm_g = norm_g_ref[...]
    hc = _rms_norm(xc_ref[...], norm_g).astype(bf16)
    hp = jnp.where(is_first, 0.0, _rms_norm(xp_ref[...], norm_g)).astype(bf16)
    hn = jnp.where(is_last, 0.0, _rms_norm(xn_ref[...], norm_g)).astype(bf16)
    h_all = jnp.concatenate([hp, hc, hn], axis=0)
    h_pool = h_all[HALO - POOL_LEAD:HALO + TILE + POOL_TAIL]

    kv_all = proj(h_all, OFF_K, 2 * KV_WIDTH)
    k_all, v_all = kv_all[:, :KV_WIDTH], kv_all[:, KV_WIDTH:]
    lane = lax.broadcasted_iota(jnp.int32, (1, 2 * HEAD_DIM), 1)
    low = lane < HEAD_DIM

    def dup_heads(a):
        swapped = pltpu.roll(a, HEAD_DIM, axis=1)
        return (jnp.where(low, a, swapped).astype(bf16), jnp.where(low, swapped, a).astype(bf16))

    k_dup = dup_heads(k_all)
    v_dup = dup_heads(v_all)

    q = proj(hc, OFF_Q, ATTN_WIDTH) * (HEAD_DIM ** -0.5)
    zero = jnp.zeros((), f32)

    def scores(hk, j):
        rows = slice(j * BLOCK, (j + 1) * BLOCK)
        pieces = []
        for pair in range(GROUP // 2):
            col = (hk * GROUP // 2 + pair) * 2 * HEAD_DIM
            qp = q[rows, col:col + 2 * HEAD_DIM]
            pieces.append(jnp.where(low, qp, zero))
            pieces.append(jnp.where(low, zero, qp))
        lhs = jnp.concatenate(pieces, axis=0).astype(bf16)
        kwin = k_dup[hk][j * BLOCK:j * BLOCK + 3 * BLOCK]
        s = lax.dot_general(lhs, kwin, (((1,), (1,)), ((), ())), preferred_element_type=f32)
        if j == 0:
            variant = jnp.where(is_first, 1, 0)
        elif j == n_blk - 1:
            variant = jnp.where(is_last, 2, 0)
        else:
            variant = 0
        return s + bias_ref[variant, hk]

    def softmax(s, hk):
        sink_col = sink_ref[hk]
        m = jnp.maximum(jnp.max(s, axis=-1, keepdims=True), sink_col)
        p = jnp.exp(s - m)
        denom = jnp.sum(p, axis=-1, keepdims=True) + jnp.exp(sink_col - m)
        return p.astype(bf16), denom

    def weighted_values(p, denom, hk, j):
        rows = slice(j * BLOCK, (j + 1) * BLOCK)
        vwin = v_dup[hk][j * BLOCK:j * BLOCK + 3 * BLOCK]
        o = jnp.dot(p, vwin, preferred_element_type=f32) / denom
        for pair in range(GROUP // 2):
            even = o[(2 * pair) * BLOCK:(2 * pair + 1) * BLOCK]
            odd = o[(2 * pair + 1) * BLOCK:(2 * pair + 2) * BLOCK]
            col = (hk * GROUP // 2 + pair) * 2 * HEAD_DIM
            attn_scr[rows, col:col + 2 * HEAD_DIM] = jnp.where(low, even, odd)

    def pooled_group(u, gi):
        w = POOL_WINDOWS[gi]
        r0, n = POOL_LEAD, TILE
        if w == 2:
            wsum = u[r0 - 1:r0 - 1 + n] + u[r0:r0 + n]
        else:
            c = u[0:n + 32] + u[1:n + 33]
            if w == 4:
                wsum = c[r0 - 2:r0 - 2 + n] + c[r0:r0 + n]
            else:
                e = c[0:n + 24] + c[2:n + 26]
                if w == 8:
                    wsum = e[r0 - 4:r0 - 4 + n] + e[r0:r0 + n]
                else:
                    f = e[0:n + 16] + e[4:n + 20]
                    wsum = f[r0 - 8:r0 - 8 + n] + f[r0:r0 + n]
        edge = POOL_EDGE_ROWS
        head = wsum[:edge] * cnt_ref[0, jnp.where(is_first, 1, 0), gi]
        body = wsum[edge:n - edge] * (1.0 / w)
        tail = wsum[n - edge:] * cnt_ref[1, jnp.where(is_last, 1, 0), gi]
        pooled = jnp.concatenate([head, body, tail], axis=0) - u[r0:r0 + n]
        return jnp.dot(pooled.astype(bf16), pool_w_ref[gi], preferred_element_type=f32)

    work = [(hk, j) for j in range(n_blk) for hk in range(N_KV_HEADS)]
    s_list = [scores(hk, j) for hk, j in work]
    fillers = (proj_chunks(h_pool, OFF_U, POOL_WIDTH) + proj_chunks(hc, OFF_ZB, POOL_WIDTH)
               + proj_chunks(hc, OFF_ZA, ATTN_WIDTH))
    filled = []
    p_list = []
    for n, (s, (hk, j)) in enumerate(zip(s_list, work)):
        if n < len(fillers):
            filled.append(fillers[n]())
        p_list.append(softmax(s, hk))
    filled += [f() for f in fillers[len(work):]]
    n_c = POOL_WIDTH // MXU_COLS
    u_all = jnp.concatenate(filled[0:n_c], axis=1)
    z_b = jnp.concatenate(filled[n_c:2 * n_c], axis=1)
    z_a = jnp.concatenate(filled[2 * n_c:], axis=1)

    for (p, denom), (hk, j) in zip(p_list, work):
        weighted_values(p, denom, hk, j)
    mixed = [pooled_group(u_all[:, gi * POOL_GROUP:(gi + 1) * POOL_GROUP], gi)
             for gi in range(len(POOL_WINDOWS))]
    mixed = jnp.concatenate(mixed, axis=1) * pool_scale_ref[...]
    g_a = proj(hc, OFF_GA, D_MODEL)
    gated_b = (mixed * _silu(z_b)).astype(bf16)
    y_b = jnp.dot(gated_b, w_b_ref[...], preferred_element_type=f32)
    gated_a = (attn_scr[...] * _silu(z_a)).astype(bf16)
    y_a = jnp.dot(gated_a, w_a_ref[...], preferred_element_type=f32)
    g_b = proj(hc, OFF_GB, D_MODEL)
    t_a = jax.nn.sigmoid(g_a) * y_a

    final_g = final_g_ref[...]
    for r in range(0, TILE, TAIL_ROWS):
        rows = slice(r, r + TAIL_ROWS)
        merged = (t_a[rows] + jax.nn.sigmoid(g_b[rows]) * y_b[rows]).astype(bf16)
        y = xc_ref[rows, :] + jnp.dot(merged, w_out_ref[...], preferred_element_type=f32)
        out_ref[rows, :] = _rms_norm(y, final_g)


def kernel(x, norm_g, w_in, attn_sink, pool_w, pool_scale, w_branch_a, w_branch_b, w_out, final_norm_g):
    batch, seq_len, d_model = x.shape
    depth = norm_g.shape[0]
    assert depth == 1, "the final RMSNorm is fused into the single layer's kernel"
    assert d_model == D_MODEL and w_in.shape[-1] == IN_WIDTH
    assert seq_len % TILE == 0 and seq_len // BLOCK >= 2 and seq_len >= 2 * POOL_EDGE_ROWS
    assert POOL_LEAD >= max(POOL_WINDOWS) // 2 and POOL_EDGE_ROWS >= max(POOL_WINDOWS) // 2
    bf16 = jnp.bfloat16
    bias = jnp.asarray(_attention_bias())
    inv_count = jnp.asarray(_pool_inv_count(seq_len))
    n_tiles = seq_len // TILE
    blk_per_tile = TILE // BLOCK
    last_blk = seq_len // BLOCK - 1

    def resident(shape):
        return pl.BlockSpec(shape, lambda b, i: (0,) * len(shape), pipeline_mode=pl.Buffered(1))

    call = pl.pallas_call(
        _block_kernel,
        grid=(batch, n_tiles),
        in_specs=[
            pl.BlockSpec((None, TILE, D_MODEL), lambda b, i: (b, i, 0)),
            pl.BlockSpec((None, HALO, D_MODEL),
                         lambda b, i: (b, jnp.maximum(i * blk_per_tile - 1, 0), 0)),
            pl.BlockSpec((None, HALO, D_MODEL),
                         lambda b, i: (b, jnp.minimum((i + 1) * blk_per_tile, last_blk), 0)),
            resident((1, D_MODEL)),
            resident((D_MODEL, IN_WIDTH)),
            resident((N_KV_HEADS, GROUP * BLOCK, 1)),
            resident((3, N_KV_HEADS, GROUP * BLOCK, 3 * BLOCK)),
            resident(inv_count.shape),
            resident((len(POOL_WINDOWS), POOL_GROUP, POOL_GROUP)),
            resident((1, POOL_WIDTH)),
            resident((ATTN_WIDTH, D_MODEL)),
            resident((POOL_WIDTH, D_MODEL)),
            resident((D_MODEL, D_MODEL)),
            resident((1, D_MODEL)),
        ],
        out_specs=pl.BlockSpec((None, TILE, D_MODEL), lambda b, i: (b, i, 0)),
        out_shape=jax.ShapeDtypeStruct(x.shape, x.dtype),
        scratch_shapes=[
            pltpu.VMEM((TILE, ATTN_WIDTH), jnp.float32),
        ],
        compiler_params=pltpu.CompilerParams(
            dimension_semantics=("arbitrary", "arbitrary"),
            vmem_limit_bytes=VMEM_LIMIT_BYTES),
        name="hybrid_block",
    )

    for l in range(depth):
        sink_col = jnp.repeat(attn_sink[l].astype(jnp.float32), BLOCK).reshape(N_KV_HEADS, GROUP * BLOCK, 1)
        x = call(x, x, x,
                 norm_g[l].reshape(1, D_MODEL), w_in[l].astype(bf16), sink_col, bias, inv_count,
                 pool_w[l].astype(bf16), pool_scale[l].reshape(1, POOL_WIDTH),
                 w_branch_a[l].astype(bf16), w_branch_b[l].astype(bf16), w_out[l].astype(bf16),
                 final_norm_g.reshape(1, D_MODEL))
    return x
```

```python
import functools

import numpy as np
import jax
import jax.numpy as jnp
from jax import lax
from jax.experimental import pallas as pl
from jax.experimental.pallas import tpu as pltpu

D_MODEL = 1024
N_Q_HEADS = 8
N_KV_HEADS = 2
GROUP = N_Q_HEADS // N_KV_HEADS
HEAD_DIM = 64
ATTN_WIDTH = N_Q_HEADS * HEAD_DIM
KV_WIDTH = N_KV_HEADS * HEAD_DIM
WINDOW = 128
BLOCK = 128
POOL_WIDTH = 512
POOL_WINDOWS = (2, 4, 8, 16)
POOL_GROUP = 128
EPS = 1e-6
NEG_INF = -1e30
LOG2_E = 1.4426950408889634

OFF_Q = 0
OFF_K = OFF_Q + ATTN_WIDTH
OFF_V = OFF_K + KV_WIDTH
OFF_ZA = OFF_V + KV_WIDTH
OFF_U = OFF_ZA + ATTN_WIDTH
OFF_ZB = OFF_U + POOL_WIDTH
OFF_GA = OFF_ZB + POOL_WIDTH
OFF_GB = OFF_GA + D_MODEL
IN_WIDTH = OFF_GB + D_MODEL

TILE = 512
HALO = BLOCK
ROWS = TILE + 2 * HALO
MXU_COLS = 256
TAIL_ROWS = 128
SCORE_LEAD = 2
FILL_PER_STEP = 2
POOL_EDGE_ROWS = 16
POOL_LEAD = 16
POOL_TAIL = 32
VMEM_LIMIT_BYTES = 56 * 1024 * 1024


def _attention_bias():
    r = np.arange(BLOCK)[:, None]
    c = np.arange(3 * BLOCK)[None, :]
    dist = np.abs(r - (c - BLOCK))
    in_window = dist <= WINDOW
    slopes = np.exp2(-8.0 * np.arange(1, N_Q_HEADS + 1, dtype=np.float64) / N_Q_HEADS)
    alibi = -slopes[:, None, None] * dist[None].astype(np.float64)
    valid = [np.ones_like(c, bool), c >= BLOCK, c < 2 * BLOCK]
    out = np.stack([np.where(in_window & v, alibi * LOG2_E, NEG_INF) for v in valid])
    return out.reshape(3, N_KV_HEADS, GROUP * BLOCK, 3 * BLOCK).astype(np.float32)


def _pool_inv_count(seq_len):
    out = np.zeros((2, 2, len(POOL_WINDOWS), POOL_EDGE_ROWS, 1), np.float64)
    for gi, w in enumerate(POOL_WINDOWS):
        half = w // 2
        out[:, 0, gi] = 1.0 / w
        for end, pos in enumerate([np.arange(POOL_EDGE_ROWS), seq_len - POOL_EDGE_ROWS + np.arange(POOL_EDGE_ROWS)]):
            count = np.minimum(pos + half, seq_len) - np.maximum(pos - half, 0)
            out[end, 1, gi, :, 0] = 1.0 / count
    return np.broadcast_to(out, out.shape[:-1] + (POOL_GROUP,)).astype(np.float32)


def _rms_norm(x, g):
    ms = jnp.mean(x * x, axis=-1, keepdims=True)
    return x * lax.rsqrt(ms + EPS) * g


def _silu(x):
    return x * jax.nn.sigmoid(x)


def _block_kernel(x_cur_ref, x_nxt_ref, x_after_ref, norm_g_ref, w_in_ref, sink_ref, bias_ref, cnt_ref,
                  pool_w_ref, pool_scale_ref, w_a_ref, w_b_ref, w_out_ref, final_g_ref,
                  out_ref, h_scr, k_scr, v_scr, q_scr, attn_scr, *, tiles_per_seq):
    f32, bf16 = jnp.float32, jnp.bfloat16
    step = pl.program_id(0)
    i = lax.rem(step, tiles_per_seq)
    n_blk = TILE // BLOCK
    is_first = i == 0
    is_last = i == tiles_per_seq - 1
    norm_g = norm_g_ref[...]
    lane = lax.broadcasted_iota(jnp.int32, (1, 2 * HEAD_DIM), 1)
    low = lane < HEAD_DIM

    def proj(h, off, width):
        return jnp.dot(h, w_in_ref[:, off:off + width], preferred_element_type=f32)

    def proj_chunks(rows, off, width, act=lambda a: a):
        return [lambda c=c: act(proj(h_scr[rows, :], off + c, MXU_COLS)) for c in range(0, width, MXU_COLS)]

    def normed(x_ref, r):
        return _rms_norm(x_ref[r:r + BLOCK, :], norm_g).astype(bf16)

    def store_kv(h_rows, dst):
        n = h_rows.shape[0]
        kv = proj(h_rows, OFF_K, 2 * KV_WIDTH)
        for scr, a in ((k_scr, kv[:, :KV_WIDTH]), (v_scr, kv[:, KV_WIDTH:])):
            swapped = pltpu.roll(a, HEAD_DIM, axis=1)
            scr[0, dst:dst + n, :] = jnp.where(low, a, swapped).astype(bf16)
            scr[1, dst:dst + n, :] = jnp.where(low, swapped, a).astype(bf16)

    def store_q(h_rows):
        q_scr[...] = (proj(h_rows, OFF_Q, ATTN_WIDTH) * (HEAD_DIM ** -0.5 * LOG2_E)).astype(bf16)

    @pl.when(step == 0)
    def _prepare_first_tile():
        zeros = jnp.zeros((HALO, D_MODEL), bf16)
        h_new = jnp.concatenate([normed(x_cur_ref, r) for r in range(0, TILE, BLOCK)]
                                + [normed(x_nxt_ref, 0)], axis=0)
        h_scr[0:HALO, :] = zeros
        h_scr[HALO:ROWS, :] = h_new
        store_kv(zeros, 0)
        store_kv(h_new, HALO)
        store_q(h_new[0:TILE])

    zero = jnp.zeros((), bf16)

    def scores(hk, j):
        rows = slice(j * BLOCK, (j + 1) * BLOCK)
        pieces = []
        for pair in range(GROUP // 2):
            col = (hk * GROUP // 2 + pair) * 2 * HEAD_DIM
            qp = q_scr[rows, col:col + 2 * HEAD_DIM]
            pieces.append(jnp.where(low, qp, zero))
            pieces.append(jnp.where(low, zero, qp))
        lhs = jnp.concatenate(pieces, axis=0)
        kwin = k_scr[hk, j * BLOCK:j * BLOCK + 3 * BLOCK, :]
        s = lax.dot_general(lhs, kwin, (((1,), (1,)), ((), ())), preferred_element_type=f32)
        if j == 0:
            variant = jnp.where(is_first, 1, 0)
        elif j == n_blk - 1:
            variant = jnp.where(is_last, 2, 0)
        else:
            variant = 0
        return s + bias_ref[variant, hk]

    def softmax(s, hk):
        sink_col = sink_ref[hk]
        m = jnp.maximum(jnp.max(s, axis=-1, keepdims=True), sink_col)
        p = jnp.exp2(s - m)
        denom = jnp.sum(p, axis=-1, keepdims=True) + jnp.exp2(sink_col - m)
        return p.astype(bf16), denom

    def weighted_values(p, denom, hk, j):
        rows = slice(j * BLOCK, (j + 1) * BLOCK)
        vwin = v_scr[hk, j * BLOCK:j * BLOCK + 3 * BLOCK, :]
        o = jnp.dot(p, vwin, preferred_element_type=f32) / denom
        for pair in range(GROUP // 2):
            even = o[(2 * pair) * BLOCK:(2 * pair + 1) * BLOCK]
            odd = o[(2 * pair + 1) * BLOCK:(2 * pair + 2) * BLOCK]
            col = (hk * GROUP // 2 + pair) * 2 * HEAD_DIM
            attn_scr[rows, col:col + 2 * HEAD_DIM] = jnp.where(low, even, odd)

    def pooled_group(u, gi):
        w = POOL_WINDOWS[gi]
        r0, n = POOL_LEAD, TILE
        if w == 2:
            wsum = u[r0 - 1:r0 - 1 + n] + u[r0:r0 + n]
        else:
            c = u[0:n + 32] + u[1:n + 33]
            if w == 4:
                wsum = c[r0 - 2:r0 - 2 + n] + c[r0:r0 + n]
            else:
                e = c[0:n + 24] + c[2:n + 26]
                if w == 8:
                    wsum = e[r0 - 4:r0 - 4 + n] + e[r0:r0 + n]
                else:
                    f = e[0:n + 16] + e[4:n + 20]
                    wsum = f[r0 - 8:r0 - 8 + n] + f[r0:r0 + n]
        edge = POOL_EDGE_ROWS
        head = wsum[:edge] * cnt_ref[0, jnp.where(is_first, 1, 0), gi]
        body = wsum[edge:n - edge] * (1.0 / w)
        tail = wsum[n - edge:] * cnt_ref[1, jnp.where(is_last, 1, 0), gi]
        pooled = jnp.concatenate([head, body, tail], axis=0) - u[r0:r0 + n]
        return jnp.dot(pooled.astype(bf16), pool_w_ref[gi], preferred_element_type=f32)

    centre = slice(HALO, HALO + TILE)
    pool_rows = slice(HALO - POOL_LEAD, HALO + TILE + POOL_TAIL)
    work = [(hk, j) for j in range(n_blk) for hk in range(N_KV_HEADS)]
    fillers = (proj_chunks(pool_rows, OFF_U, POOL_WIDTH) + proj_chunks(centre, OFF_ZB, POOL_WIDTH)
               + proj_chunks(centre, OFF_ZA, ATTN_WIDTH)
               + proj_chunks(centre, OFF_GA, D_MODEL, jax.nn.sigmoid)
               + proj_chunks(centre, OFF_GB, D_MODEL, jax.nn.sigmoid))
    filled = []
    s_list = [scores(*work[n]) for n in range(SCORE_LEAD)]
    p_prev = None
    for n, (hk, j) in enumerate(work):
        filled += [f() for f in fillers[FILL_PER_STEP * n:FILL_PER_STEP * (n + 1)]]
        if n + SCORE_LEAD < len(work):
            s_list.append(scores(*work[n + SCORE_LEAD]))
        p_cur = softmax(s_list[n], hk)
        if p_prev is not None:
            weighted_values(*p_prev, *work[n - 1])
        p_prev = p_cur
    filled += [f() for f in fillers[FILL_PER_STEP * len(work):]]
    weighted_values(*p_prev, *work[-1])

    def take(width):
        chunks = [filled.pop(0) for _ in range(width // MXU_COLS)]
        return jnp.concatenate(chunks, axis=1)

    u_all, z_b, z_a = take(POOL_WIDTH), take(POOL_WIDTH), take(ATTN_WIDTH)
    gate_a, gate_b = take(D_MODEL), take(D_MODEL)
    u_all = jnp.concatenate([jnp.where(is_first, 0.0, u_all[:POOL_LEAD]),
                             u_all[POOL_LEAD:POOL_LEAD + TILE],
                             jnp.where(is_last, 0.0, u_all[POOL_LEAD + TILE:])], axis=0)

    gated_a = (attn_scr[...] * _silu(z_a)).astype(bf16)
    y_a = jnp.dot(gated_a, w_a_ref[...], preferred_element_type=f32)
    mixed = [pooled_group(u_all[:, gi * POOL_GROUP:(gi + 1) * POOL_GROUP], gi)
             for gi in range(len(POOL_WINDOWS))]
    mixed = jnp.concatenate(mixed, axis=1) * pool_scale_ref[...]
    gated_b = (mixed * _silu(z_b)).astype(bf16)
    y_b = jnp.dot(gated_b, w_b_ref[...], preferred_element_type=f32)
    t_a = gate_a * y_a

    final_g = final_g_ref[...]
    for r in range(0, TILE, TAIL_ROWS):
        rows = slice(r, r + TAIL_ROWS)
        merged = (t_a[rows] + gate_b[rows] * y_b[rows]).astype(bf16)
        y = x_cur_ref[rows, :] + jnp.dot(merged, w_out_ref[...], preferred_element_type=f32)
        out_ref[rows, :] = _rms_norm(y, final_g)

    h_new = jnp.concatenate([normed(x_nxt_ref, r) for r in range(HALO, TILE, BLOCK)]
                            + [normed(x_after_ref, 0)], axis=0)
    h_keep = h_scr[TILE:ROWS, :]
    k_keep, v_keep = k_scr[:, TILE:ROWS, :], v_scr[:, TILE:ROWS, :]
    h_scr[0:2 * HALO, :] = h_keep
    h_scr[2 * HALO:ROWS, :] = h_new
    k_scr[:, 0:2 * HALO, :] = k_keep
    v_scr[:, 0:2 * HALO, :] = v_keep
    for r in range(0, TILE, 2 * HALO):
        store_kv(h_new[r:r + 2 * HALO], 2 * HALO + r)
    store_q(jnp.concatenate([h_keep[HALO:], h_new[:TILE - HALO]], axis=0))


def kernel(x, norm_g, w_in, attn_sink, pool_w, pool_scale, w_branch_a, w_branch_b, w_out, final_norm_g):
    batch, seq_len, d_model = x.shape
    depth = norm_g.shape[0]
    assert depth == 1, "the final RMSNorm is fused into the single layer's kernel"
    assert d_model == D_MODEL and w_in.shape[-1] == IN_WIDTH
    assert seq_len % TILE == 0 and seq_len // BLOCK >= 2 and seq_len >= 2 * POOL_EDGE_ROWS
    assert POOL_LEAD >= max(POOL_WINDOWS) // 2 and POOL_EDGE_ROWS >= max(POOL_WINDOWS) // 2
    assert TILE // BLOCK >= 2 and TILE // TAIL_ROWS >= 2
    bf16 = jnp.bfloat16
    bias = jnp.asarray(_attention_bias())
    inv_count = jnp.asarray(_pool_inv_count(seq_len))
    tiles_per_seq = seq_len // TILE
    n_steps = batch * tiles_per_seq
    blk_per_tile = TILE // BLOCK
    last_blk = n_steps * blk_per_tile - 1

    def resident(shape):
        return pl.BlockSpec(shape, lambda s: (0,) * len(shape), pipeline_mode=pl.Buffered(1))

    call = pl.pallas_call(
        functools.partial(_block_kernel, tiles_per_seq=tiles_per_seq),
        grid=(n_steps,),
        in_specs=[
            pl.BlockSpec((TILE, D_MODEL), lambda s: (s, 0)),
            pl.BlockSpec((TILE, D_MODEL), lambda s: (jnp.minimum(s + 1, n_steps - 1), 0)),
            pl.BlockSpec((BLOCK, D_MODEL), lambda s: (jnp.minimum((s + 2) * blk_per_tile, last_blk), 0)),
            resident((1, D_MODEL)),
            resident((D_MODEL, IN_WIDTH)),
            resident((N_KV_HEADS, GROUP * BLOCK, 1)),
            resident((3, N_KV_HEADS, GROUP * BLOCK, 3 * BLOCK)),
            resident(inv_count.shape),
            resident((len(POOL_WINDOWS), POOL_GROUP, POOL_GROUP)),
            resident((1, POOL_WIDTH)),
            resident((ATTN_WIDTH, D_MODEL)),
            resident((POOL_WIDTH, D_MODEL)),
            resident((D_MODEL, D_MODEL)),
            resident((1, D_MODEL)),
        ],
        out_specs=pl.BlockSpec((TILE, D_MODEL), lambda s: (s, 0)),
        out_shape=jax.ShapeDtypeStruct((batch * seq_len, d_model), x.dtype),
        scratch_shapes=[
            pltpu.VMEM((ROWS, D_MODEL), bf16),
            pltpu.VMEM((N_KV_HEADS, ROWS, 2 * HEAD_DIM), bf16),
            pltpu.VMEM((N_KV_HEADS, ROWS, 2 * HEAD_DIM), bf16),
            pltpu.VMEM((TILE, ATTN_WIDTH), bf16),
            pltpu.VMEM((TILE, ATTN_WIDTH), jnp.float32),
        ],
        compiler_params=pltpu.CompilerParams(
            dimension_semantics=("arbitrary",),
            vmem_limit_bytes=VMEM_LIMIT_BYTES),
        name="hybrid_block",
    )

    x2d = x.reshape(batch * seq_len, d_model)
    sink_col = jnp.repeat(attn_sink[0].astype(jnp.float32) * LOG2_E, BLOCK).reshape(N_KV_HEADS, GROUP * BLOCK, 1)
    y2d = call(x2d, x2d, x2d,
               norm_g[0].reshape(1, D_MODEL), w_in[0].astype(bf16), sink_col, bias, inv_count,
               pool_w[0].astype(bf16), pool_scale[0].reshape(1, POOL_WIDTH),
               w_branch_a[0].astype(bf16), w_branch_b[0].astype(bf16), w_out[0].astype(bf16),
               final_norm_g.reshape(1, D_MODEL))
    return y2d.reshape(batch, seq_len, d_model)
```

```python
import functools

import numpy as np
import jax
import jax.numpy as jnp
from jax import lax
from jax.experimental import pallas as pl
from jax.experimental.pallas import tpu as pltpu

D_MODEL = 1024
N_Q_HEADS = 8
N_KV_HEADS = 2
GROUP = N_Q_HEADS // N_KV_HEADS
HEAD_DIM = 64
ATTN_WIDTH = N_Q_HEADS * HEAD_DIM
KV_WIDTH = N_KV_HEADS * HEAD_DIM
WINDOW = 128
BLOCK = 128
POOL_WIDTH = 512
POOL_WINDOWS = (2, 4, 8, 16)
POOL_GROUP = 128
EPS = 1e-6
NEG_INF = -1e30
LOG2_E = 1.4426950408889634

OFF_Q = 0
OFF_K = OFF_Q + ATTN_WIDTH
OFF_V = OFF_K + KV_WIDTH
OFF_ZA = OFF_V + KV_WIDTH
OFF_U = OFF_ZA + ATTN_WIDTH
OFF_ZB = OFF_U + POOL_WIDTH
OFF_GA = OFF_ZB + POOL_WIDTH
OFF_GB = OFF_GA + D_MODEL
IN_WIDTH = OFF_GB + D_MODEL

TILE = 512
HALO = BLOCK
ROWS = TILE + 2 * HALO
MXU_COLS = 256
TAIL_ROWS = 256
SCORE_LEAD = 2
FILL_PER_STEP = 1
POOL_EDGE_ROWS = 16
POOL_LEAD = 16
POOL_TAIL = 32
VMEM_LIMIT_BYTES = 56 * 1024 * 1024


def _attention_bias():
    r = np.arange(BLOCK)[:, None]
    c = np.arange(3 * BLOCK)[None, :]
    dist = np.abs(r - (c - BLOCK))
    in_window = dist <= WINDOW
    slopes = np.exp2(-8.0 * np.arange(1, N_Q_HEADS + 1, dtype=np.float64) / N_Q_HEADS)
    alibi = -slopes[:, None, None] * dist[None].astype(np.float64)
    valid = [np.ones_like(c, bool), c >= BLOCK, c < 2 * BLOCK]
    out = np.stack([np.where(in_window & v, alibi * LOG2_E, NEG_INF) for v in valid])
    return out.reshape(3, N_KV_HEADS, GROUP * BLOCK, 3 * BLOCK).astype(np.float32)


def _pool_inv_count(seq_len):
    out = np.zeros((2, 2, len(POOL_WINDOWS), POOL_EDGE_ROWS, 1), np.float64)
    for gi, w in enumerate(POOL_WINDOWS):
        half = w // 2
        out[:, 0, gi] = 1.0 / w
        for end, pos in enumerate([np.arange(POOL_EDGE_ROWS), seq_len - POOL_EDGE_ROWS + np.arange(POOL_EDGE_ROWS)]):
            count = np.minimum(pos + half, seq_len) - np.maximum(pos - half, 0)
            out[end, 1, gi, :, 0] = 1.0 / count
    return np.broadcast_to(out, out.shape[:-1] + (POOL_GROUP,)).astype(np.float32)


def _rms_norm(x, g):
    ms = jnp.mean(x * x, axis=-1, keepdims=True)
    return x * lax.rsqrt(ms + EPS) * g


def _sigmoid(x):
    return 0.5 * jnp.tanh(0.5 * x) + 0.5


def _silu(x):
    return x * _sigmoid(x)


def _block_kernel(x_cur_ref, x_nxt_ref, x_after_ref, norm_g_ref, w_in_ref, sink_ref, bias_ref, cnt_ref,
                  pool_w_ref, pool_scale_ref, w_a_ref, w_b_ref, w_out_ref, final_g_ref,
                  out_ref, h_scr, k_scr, v_scr, q_scr, attn_scr, *, tiles_per_seq):
    f32, bf16 = jnp.float32, jnp.bfloat16
    step = pl.program_id(0)
    i = lax.rem(step, tiles_per_seq)
    n_blk = TILE // BLOCK
    is_first = i == 0
    is_last = i == tiles_per_seq - 1
    norm_g = norm_g_ref[...]
    lane = lax.broadcasted_iota(jnp.int32, (1, 2 * HEAD_DIM), 1)
    low = lane < HEAD_DIM

    def proj(h, off, width):
        return jnp.dot(h, w_in_ref[:, off:off + width], preferred_element_type=f32)

    def proj_chunks(rows, off, width, act=lambda a: a):
        return [lambda c=c: act(proj(h_scr[rows, :], off + c, MXU_COLS)) for c in range(0, width, MXU_COLS)]

    def normed(x_ref, r):
        return _rms_norm(x_ref[r:r + BLOCK, :], norm_g).astype(bf16)

    def store_kv(h_rows, dst):
        n = h_rows.shape[0]
        kv = proj(h_rows, OFF_K, 2 * KV_WIDTH)
        for scr, a in ((k_scr, kv[:, :KV_WIDTH]), (v_scr, kv[:, KV_WIDTH:])):
            swapped = pltpu.roll(a, HEAD_DIM, axis=1)
            scr[0, dst:dst + n, :] = jnp.where(low, a, swapped).astype(bf16)
            scr[1, dst:dst + n, :] = jnp.where(low, swapped, a).astype(bf16)

    def store_q(h_rows):
        q_scr[...] = (proj(h_rows, OFF_Q, ATTN_WIDTH) * (HEAD_DIM ** -0.5 * LOG2_E)).astype(bf16)

    @pl.when(step == 0)
    def _prepare_first_tile():
        zeros = jnp.zeros((HALO, D_MODEL), bf16)
        h_new = jnp.concatenate([normed(x_cur_ref, r) for r in range(0, TILE, BLOCK)]
                                + [normed(x_nxt_ref, 0)], axis=0)
        h_scr[0:HALO, :] = zeros
        h_scr[HALO:ROWS, :] = h_new
        store_kv(zeros, 0)
        store_kv(h_new, HALO)
        store_q(h_new[0:TILE])

    zero = jnp.zeros((), bf16)

    def scores(hk, j):
        rows = slice(j * BLOCK, (j + 1) * BLOCK)
        pieces = []
        for pair in range(GROUP // 2):
            col = (hk * GROUP // 2 + pair) * 2 * HEAD_DIM
            qp = q_scr[rows, col:col + 2 * HEAD_DIM]
            pieces.append(jnp.where(low, qp, zero))
            pieces.append(jnp.where(low, zero, qp))
        lhs = jnp.concatenate(pieces, axis=0)
        kwin = k_scr[hk, j * BLOCK:j * BLOCK + 3 * BLOCK, :]
        s = lax.dot_general(lhs, kwin, (((1,), (1,)), ((), ())), preferred_element_type=f32)
        if j == 0:
            variant = jnp.where(is_first, 1, 0)
        elif j == n_blk - 1:
            variant = jnp.where(is_last, 2, 0)
        else:
            variant = 0
        return s + bias_ref[variant, hk]

    def softmax(s, hk):
        sink_col = sink_ref[hk]
        m = jnp.maximum(jnp.max(s, axis=-1, keepdims=True), sink_col)
        p = jnp.exp2(s - m)
        denom = jnp.sum(p, axis=-1, keepdims=True) + jnp.exp2(sink_col - m)
        return p.astype(bf16), denom

    def weighted_values(p, denom, hk, j):
        rows = slice(j * BLOCK, (j + 1) * BLOCK)
        vwin = v_scr[hk, j * BLOCK:j * BLOCK + 3 * BLOCK, :]
        o = jnp.dot(p, vwin, preferred_element_type=f32) / denom
        for pair in range(GROUP // 2):
            even = o[(2 * pair) * BLOCK:(2 * pair + 1) * BLOCK]
            odd = o[(2 * pair + 1) * BLOCK:(2 * pair + 2) * BLOCK]
            col = (hk * GROUP // 2 + pair) * 2 * HEAD_DIM
            attn_scr[rows, col:col + 2 * HEAD_DIM] = jnp.where(low, even, odd)

    def pooled_group(u, gi):
        w = POOL_WINDOWS[gi]
        r0, n = POOL_LEAD, TILE
        if w == 2:
            wsum = u[r0 - 1:r0 - 1 + n] + u[r0:r0 + n]
        else:
            c = u[0:n + 32] + u[1:n + 33]
            if w == 4:
                wsum = c[r0 - 2:r0 - 2 + n] + c[r0:r0 + n]
            else:
                e = c[0:n + 24] + c[2:n + 26]
                if w == 8:
                    wsum = e[r0 - 4:r0 - 4 + n] + e[r0:r0 + n]
                else:
                    f = e[0:n + 16] + e[4:n + 20]
                    wsum = f[r0 - 8:r0 - 8 + n] + f[r0:r0 + n]
        edge = POOL_EDGE_ROWS
        head = wsum[:edge] * cnt_ref[0, jnp.where(is_first, 1, 0), gi]
        body = wsum[edge:n - edge] * (1.0 / w)
        tail = wsum[n - edge:] * cnt_ref[1, jnp.where(is_last, 1, 0), gi]
        pooled = jnp.concatenate([head, body, tail], axis=0) - u[r0:r0 + n]
        return jnp.dot(pooled.astype(bf16), pool_w_ref[gi], preferred_element_type=f32)

    centre = slice(HALO, HALO + TILE)
    pool_rows = slice(HALO - POOL_LEAD, HALO + TILE + POOL_TAIL)
    work = [(hk, j) for j in range(n_blk) for hk in range(N_KV_HEADS)]
    fillers = (proj_chunks(centre, OFF_GA, D_MODEL, _sigmoid)
               + proj_chunks(centre, OFF_GB, D_MODEL, _sigmoid)
               + proj_chunks(centre, OFF_ZA, ATTN_WIDTH)
               + proj_chunks(pool_rows, OFF_U, POOL_WIDTH)
               + proj_chunks(centre, OFF_ZB, POOL_WIDTH))
    filled = []
    s_list = [scores(*work[n]) for n in range(SCORE_LEAD)]
    p_prev = None
    for n, (hk, j) in enumerate(work):
        filled += [f() for f in fillers[FILL_PER_STEP * n:FILL_PER_STEP * (n + 1)]]
        if n + SCORE_LEAD < len(work):
            s_list.append(scores(*work[n + SCORE_LEAD]))
        p_cur = softmax(s_list[n], hk)
        if p_prev is not None:
            weighted_values(*p_prev, *work[n - 1])
        p_prev = p_cur
    filled += [f() for f in fillers[FILL_PER_STEP * len(work):]]
    weighted_values(*p_prev, *work[-1])

    def take(width):
        chunks = [filled.pop(0) for _ in range(width // MXU_COLS)]
        return jnp.concatenate(chunks, axis=1)

    gate_a = take(D_MODEL)
    gate_b = take(D_MODEL)
    z_a = take(ATTN_WIDTH)
    u_all = take(POOL_WIDTH)
    z_b = take(POOL_WIDTH)
    u_all = jnp.concatenate([jnp.where(is_first, 0.0, u_all[:POOL_LEAD]),
                             u_all[POOL_LEAD:POOL_LEAD + TILE],
                             jnp.where(is_last, 0.0, u_all[POOL_LEAD + TILE:])], axis=0)

    gated_a = (attn_scr[...] * _silu(z_a)).astype(bf16)
    y_a = jnp.dot(gated_a, w_a_ref[...], preferred_element_type=f32)
    mixed = [pooled_group(u_all[:, gi * POOL_GROUP:(gi + 1) * POOL_GROUP], gi)
             for gi in range(len(POOL_WINDOWS))]
    mixed = jnp.concatenate(mixed, axis=1) * pool_scale_ref[...]
    gated_b = (mixed * _silu(z_b)).astype(bf16)
    y_b = jnp.dot(gated_b, w_b_ref[...], preferred_element_type=f32)
    t_a = gate_a * y_a

    final_g = final_g_ref[...]
    for r in range(0, TILE, TAIL_ROWS):
        rows = slice(r, r + TAIL_ROWS)
        merged = (t_a[rows] + gate_b[rows] * y_b[rows]).astype(bf16)
        y = x_cur_ref[rows, :] + jnp.dot(merged, w_out_ref[...], preferred_element_type=f32)
        out_ref[rows, :] = _rms_norm(y, final_g)

    h_new = jnp.concatenate([normed(x_nxt_ref, r) for r in range(HALO, TILE, BLOCK)]
                            + [normed(x_after_ref, 0)], axis=0)
    h_keep = h_scr[TILE:ROWS, :]
    k_keep, v_keep = k_scr[:, TILE:ROWS, :], v_scr[:, TILE:ROWS, :]
    h_scr[0:2 * HALO, :] = h_keep
    h_scr[2 * HALO:ROWS, :] = h_new
    k_scr[:, 0:2 * HALO, :] = k_keep
    v_scr[:, 0:2 * HALO, :] = v_keep
    for r in range(0, TILE, 2 * HALO):
        store_kv(h_new[r:r + 2 * HALO], 2 * HALO + r)
    store_q(jnp.concatenate([h_keep[HALO:], h_new[:TILE - HALO]], axis=0))


def kernel(x, norm_g, w_in, attn_sink, pool_w, pool_scale, w_branch_a, w_branch_b, w_out, final_norm_g):
    batch, seq_len, d_model = x.shape
    depth = norm_g.shape[0]
    assert depth == 1, "the final RMSNorm is fused into the single layer's kernel"
    assert d_model == D_MODEL and w_in.shape[-1] == IN_WIDTH
    assert seq_len % TILE == 0 and seq_len // BLOCK >= 2 and seq_len >= 2 * POOL_EDGE_ROWS
    assert POOL_LEAD >= max(POOL_WINDOWS) // 2 and POOL_EDGE_ROWS >= max(POOL_WINDOWS) // 2
    assert TILE // BLOCK >= 2 and TILE // TAIL_ROWS >= 2
    bf16 = jnp.bfloat16
    bias = jnp.asarray(_attention_bias())
    inv_count = jnp.asarray(_pool_inv_count(seq_len))
    tiles_per_seq = seq_len // TILE
    n_steps = batch * tiles_per_seq
    blk_per_tile = TILE // BLOCK
    last_blk = n_steps * blk_per_tile - 1

    def resident(shape):
        return pl.BlockSpec(shape, lambda s: (0,) * len(shape), pipeline_mode=pl.Buffered(1))

    call = pl.pallas_call(
        functools.partial(_block_kernel, tiles_per_seq=tiles_per_seq),
        grid=(n_steps,),
        in_specs=[
            pl.BlockSpec((TILE, D_MODEL), lambda s: (s, 0)),
            pl.BlockSpec((TILE, D_MODEL), lambda s: (jnp.minimum(s + 1, n_steps - 1), 0)),
            pl.BlockSpec((BLOCK, D_MODEL), lambda s: (jnp.minimum((s + 2) * blk_per_tile, last_blk), 0)),
            resident((1, D_MODEL)),
            resident((D_MODEL, IN_WIDTH)),
            resident((N_KV_HEADS, GROUP * BLOCK, 1)),
            resident((3, N_KV_HEADS, GROUP * BLOCK, 3 * BLOCK)),
            resident(inv_count.shape),
            resident((len(POOL_WINDOWS), POOL_GROUP, POOL_GROUP)),
            resident((1, POOL_WIDTH)),
            resident((ATTN_WIDTH, D_MODEL)),
            resident((POOL_WIDTH, D_MODEL)),
            resident((D_MODEL, D_MODEL)),
            resident((1, D_MODEL)),
        ],
        out_specs=pl.BlockSpec((TILE, D_MODEL), lambda s: (s, 0)),
        out_shape=jax.ShapeDtypeStruct((batch * seq_len, d_model), x.dtype),
        scratch_shapes=[
            pltpu.VMEM((ROWS, D_MODEL), bf16),
            pltpu.VMEM((N_KV_HEADS, ROWS, 2 * HEAD_DIM), bf16),
            pltpu.VMEM((N_KV_HEADS, ROWS, 2 * HEAD_DIM), bf16),
            pltpu.VMEM((TILE, ATTN_WIDTH), bf16),
            pltpu.VMEM((TILE, ATTN_WIDTH), jnp.float32),
        ],
        compiler_params=pltpu.CompilerParams(
            dimension_semantics=("arbitrary",),
            vmem_limit_bytes=VMEM_LIMIT_BYTES),
        name="hybrid_block",
    )

    x2d = x.reshape(batch * seq_len, d_model)
    sink_col = jnp.repeat(attn_sink[0].astype(jnp.float32) * LOG2_E, BLOCK).reshape(N_KV_HEADS, GROUP * BLOCK, 1)
    y2d = call(x2d, x2d, x2d,
               norm_g[0].reshape(1, D_MODEL), w_in[0].astype(bf16), sink_col, bias, inv_count,
               pool_w[0].astype(bf16), pool_scale[0].reshape(1, POOL_WIDTH),
               w_branch_a[0].astype(bf16), w_branch_b[0].astype(bf16), w_out[0].astype(bf16),
               final_norm_g.reshape(1, D_MODEL))
    return y2d.reshape(batch, seq_len, d_model)
```

```python
import functools

import numpy as np
import jax
import jax.numpy as jnp
from jax import lax
from jax.experimental import pallas as pl
from jax.experimental.pallas import tpu as pltpu

D_MODEL = 1024
N_Q_HEADS = 8
N_KV_HEADS = 2
GROUP = N_Q_HEADS // N_KV_HEADS
HEAD_DIM = 64
ATTN_WIDTH = N_Q_HEADS * HEAD_DIM
KV_WIDTH = N_KV_HEADS * HEAD_DIM
WINDOW = 128
BLOCK = 128
POOL_WIDTH = 512
POOL_WINDOWS = (2, 4, 8, 16)
POOL_GROUP = 128
EPS = 1e-6
NEG_INF = -1e30
LOG2_E = 1.4426950408889634

OFF_Q = 0
OFF_K = OFF_Q + ATTN_WIDTH
OFF_V = OFF_K + KV_WIDTH
OFF_ZA = OFF_V + KV_WIDTH
OFF_U = OFF_ZA + ATTN_WIDTH
OFF_ZB = OFF_U + POOL_WIDTH
OFF_GA = OFF_ZB + POOL_WIDTH
OFF_GB = OFF_GA + D_MODEL
IN_WIDTH = OFF_GB + D_MODEL
OFF_WOUT = IN_WIDTH
OFF_WAB = OFF_WOUT + D_MODEL
OFF_POOLW = OFF_WAB + D_MODEL
PACK_WIDTH = OFF_POOLW + POOL_GROUP

TILE = 512
HALO = BLOCK
ROWS = TILE + 2 * HALO
MXU_COLS = 256
TAIL_ROWS = 256
SCORE_LEAD = 2
FILL_PER_STEP = 1
POOL_EDGE_ROWS = 16
POOL_LEAD = 16
POOL_TAIL = 32
VMEM_LIMIT_BYTES = 56 * 1024 * 1024


def _attention_bias():
    r = np.arange(BLOCK)[:, None]
    c = np.arange(3 * BLOCK)[None, :]
    dist = np.abs(r - (c - BLOCK))
    in_window = dist <= WINDOW
    slopes = np.exp2(-8.0 * np.arange(1, N_Q_HEADS + 1, dtype=np.float64) / N_Q_HEADS)
    alibi = -slopes[:, None, None] * dist[None].astype(np.float64)
    valid = [np.ones_like(c, bool), c >= BLOCK, c < 2 * BLOCK]
    out = np.stack([np.where(in_window & v, alibi * LOG2_E, NEG_INF) for v in valid])
    return out.reshape(3, N_KV_HEADS, GROUP * BLOCK, 3 * BLOCK).astype(np.float32)


def _pool_inv_count(seq_len):
    out = np.zeros((2, 2, len(POOL_WINDOWS), POOL_EDGE_ROWS, 1), np.float64)
    for gi, w in enumerate(POOL_WINDOWS):
        half = w // 2
        out[:, 0, gi] = 1.0 / w
        for end, pos in enumerate([np.arange(POOL_EDGE_ROWS), seq_len - POOL_EDGE_ROWS + np.arange(POOL_EDGE_ROWS)]):
            count = np.minimum(pos + half, seq_len) - np.maximum(pos - half, 0)
            out[end, 1, gi, :, 0] = 1.0 / count
    return np.broadcast_to(out, out.shape[:-1] + (POOL_GROUP,)).astype(np.float32)


def _rms_norm(x, g):
    ms = jnp.mean(x * x, axis=-1, keepdims=True)
    return x * lax.rsqrt(ms + EPS) * g


def _sigmoid(x):
    return 0.5 * jnp.tanh(0.5 * x) + 0.5


def _silu(x):
    return x * _sigmoid(x)


def _block_kernel(x_cur_ref, x_nxt_ref, x_after_ref, norm_g_ref, w_ref, sink_ref, bias_ref, cnt_ref,
                  pool_scale_ref, final_g_ref,
                  out_ref, h_scr, k_scr, v_scr, q_scr, attn_scr, *, tiles_per_seq):
    f32, bf16 = jnp.float32, jnp.bfloat16
    step = pl.program_id(0)
    i = lax.rem(step, tiles_per_seq)
    n_blk = TILE // BLOCK
    is_first = i == 0
    is_last = i == tiles_per_seq - 1
    norm_g = norm_g_ref[...]
    lane = lax.broadcasted_iota(jnp.int32, (1, 2 * HEAD_DIM), 1)
    low = lane < HEAD_DIM

    def proj(h, off, width):
        return jnp.dot(h, w_ref[:, off:off + width], preferred_element_type=f32)

    def proj_chunks(rows, off, width, act=lambda a: a):
        return [lambda c=c: act(proj(h_scr[rows, :], off + c, MXU_COLS)) for c in range(0, width, MXU_COLS)]

    def normed(x_ref, r):
        return _rms_norm(x_ref[r:r + BLOCK, :], norm_g).astype(bf16)

    def store_kv(h_rows, dst):
        n = h_rows.shape[0]
        kv = proj(h_rows, OFF_K, 2 * KV_WIDTH)
        for scr, a in ((k_scr, kv[:, :KV_WIDTH]), (v_scr, kv[:, KV_WIDTH:])):
            swapped = pltpu.roll(a, HEAD_DIM, axis=1)
            scr[0, dst:dst + n, :] = jnp.where(low, a, swapped).astype(bf16)
            scr[1, dst:dst + n, :] = jnp.where(low, swapped, a).astype(bf16)

    def store_q(h_rows):
        q_scr[...] = (proj(h_rows, OFF_Q, ATTN_WIDTH) * (HEAD_DIM ** -0.5 * LOG2_E)).astype(bf16)

    @pl.when(step == 0)
    def _prepare_first_tile():
        zeros = jnp.zeros((HALO, D_MODEL), bf16)
        h_new = jnp.concatenate([normed(x_cur_ref, r) for r in range(0, TILE, BLOCK)]
                                + [normed(x_nxt_ref, 0)], axis=0)
        h_scr[0:HALO, :] = zeros
        h_scr[HALO:ROWS, :] = h_new
        store_kv(zeros, 0)
        store_kv(h_new, HALO)
        store_q(h_new[0:TILE])

    zero = jnp.zeros((), bf16)

    def scores(hk, j):
        rows = slice(j * BLOCK, (j + 1) * BLOCK)
        pieces = []
        for pair in range(GROUP // 2):
            col = (hk * GROUP // 2 + pair) * 2 * HEAD_DIM
            qp = q_scr[rows, col:col + 2 * HEAD_DIM]
            pieces.append(jnp.where(low, qp, zero))
            pieces.append(jnp.where(low, zero, qp))
        lhs = jnp.concatenate(pieces, axis=0)
        kwin = k_scr[hk, j * BLOCK:j * BLOCK + 3 * BLOCK, :]
        s = lax.dot_general(lhs, kwin, (((1,), (1,)), ((), ())), preferred_element_type=f32)
        if j == 0:
            variant = jnp.where(is_first, 1, 0)
        elif j == n_blk - 1:
            variant = jnp.where(is_last, 2, 0)
        else:
            variant = 0
        return s + bias_ref[variant, hk]

    def softmax(s, hk):
        sink_col = sink_ref[hk]
        m = jnp.maximum(jnp.max(s, axis=-1, keepdims=True), sink_col)
        p = jnp.exp2(s - m)
        denom = jnp.sum(p, axis=-1, keepdims=True) + jnp.exp2(sink_col - m)
        return p.astype(bf16), denom

    def weighted_values(p, denom, hk, j):
        rows = slice(j * BLOCK, (j + 1) * BLOCK)
        vwin = v_scr[hk, j * BLOCK:j * BLOCK + 3 * BLOCK, :]
        o = jnp.dot(p, vwin, preferred_element_type=f32) / denom
        for pair in range(GROUP // 2):
            even = o[(2 * pair) * BLOCK:(2 * pair + 1) * BLOCK]
            odd = o[(2 * pair + 1) * BLOCK:(2 * pair + 2) * BLOCK]
            col = (hk * GROUP // 2 + pair) * 2 * HEAD_DIM
            attn_scr[rows, col:col + 2 * HEAD_DIM] = jnp.where(low, even, odd)

    def pooled_group(u, gi):
        w = POOL_WINDOWS[gi]
        r0, n = POOL_LEAD, TILE
        if w == 2:
            wsum = u[r0 - 1:r0 - 1 + n] + u[r0:r0 + n]
        else:
            c = u[0:n + 32] + u[1:n + 33]
            if w == 4:
                wsum = c[r0 - 2:r0 - 2 + n] + c[r0:r0 + n]
            else:
                e = c[0:n + 24] + c[2:n + 26]
                if w == 8:
                    wsum = e[r0 - 4:r0 - 4 + n] + e[r0:r0 + n]
                else:
                    f = e[0:n + 16] + e[4:n + 20]
                    wsum = f[r0 - 8:r0 - 8 + n] + f[r0:r0 + n]
        edge = POOL_EDGE_ROWS
        head = wsum[:edge] * cnt_ref[0, jnp.where(is_first, 1, 0), gi]
        body = wsum[edge:n - edge] * (1.0 / w)
        tail = wsum[n - edge:] * cnt_ref[1, jnp.where(is_last, 1, 0), gi]
        pooled = jnp.concatenate([head, body, tail], axis=0) - u[r0:r0 + n]
        pool_w = w_ref[gi * POOL_GROUP:(gi + 1) * POOL_GROUP, OFF_POOLW:PACK_WIDTH]
        return jnp.dot(pooled.astype(bf16), pool_w, preferred_element_type=f32)

    centre = slice(HALO, HALO + TILE)
    pool_rows = slice(HALO - POOL_LEAD, HALO + TILE + POOL_TAIL)
    work = [(hk, j) for j in range(n_blk) for hk in range(N_KV_HEADS)]
    fillers = (proj_chunks(centre, OFF_GA, D_MODEL, _sigmoid)
               + proj_chunks(centre, OFF_GB, D_MODEL, _sigmoid)
               + proj_chunks(centre, OFF_ZA, ATTN_WIDTH)
               + proj_chunks(pool_rows, OFF_U, POOL_WIDTH)
               + proj_chunks(centre, OFF_ZB, POOL_WIDTH))
    filled = []
    s_list = [scores(*work[n]) for n in range(SCORE_LEAD)]
    p_prev = None
    for n, (hk, j) in enumerate(work):
        filled += [f() for f in fillers[FILL_PER_STEP * n:FILL_PER_STEP * (n + 1)]]
        if n + SCORE_LEAD < len(work):
            s_list.append(scores(*work[n + SCORE_LEAD]))
        p_cur = softmax(s_list[n], hk)
        if p_prev is not None:
            weighted_values(*p_prev, *work[n - 1])
        p_prev = p_cur
    filled += [f() for f in fillers[FILL_PER_STEP * len(work):]]
    weighted_values(*p_prev, *work[-1])

    def take(width):
        chunks = [filled.pop(0) for _ in range(width // MXU_COLS)]
        return jnp.concatenate(chunks, axis=1)

    gate_a = take(D_MODEL)
    gate_b = take(D_MODEL)
    z_a = take(ATTN_WIDTH)
    u_all = take(POOL_WIDTH)
    z_b = take(POOL_WIDTH)
    u_all = jnp.concatenate([jnp.where(is_first, 0.0, u_all[:POOL_LEAD]),
                             u_all[POOL_LEAD:POOL_LEAD + TILE],
                             jnp.where(is_last, 0.0, u_all[POOL_LEAD + TILE:])], axis=0)

    gated_a = (attn_scr[...] * _silu(z_a)).astype(bf16)
    y_a = jnp.dot(gated_a, w_ref[0:ATTN_WIDTH, OFF_WAB:OFF_POOLW], preferred_element_type=f32)
    mixed = [pooled_group(u_all[:, gi * POOL_GROUP:(gi + 1) * POOL_GROUP], gi)
             for gi in range(len(POOL_WINDOWS))]
    mixed = jnp.concatenate(mixed, axis=1) * pool_scale_ref[...]
    gated_b = (mixed * _silu(z_b)).astype(bf16)
    y_b = jnp.dot(gated_b, w_ref[ATTN_WIDTH:ATTN_WIDTH + POOL_WIDTH, OFF_WAB:OFF_POOLW], preferred_element_type=f32)
    t_a = gate_a * y_a

    final_g = final_g_ref[...]
    for r in range(0, TILE, TAIL_ROWS):
        rows = slice(r, r + TAIL_ROWS)
        merged = (t_a[rows] + gate_b[rows] * y_b[rows]).astype(bf16)
        y = x_cur_ref[rows, :] + jnp.dot(merged, w_ref[:, OFF_WOUT:OFF_WAB], preferred_element_type=f32)
        out_ref[rows, :] = _rms_norm(y, final_g)

    h_new = jnp.concatenate([normed(x_nxt_ref, r) for r in range(HALO, TILE, BLOCK)]
                            + [normed(x_after_ref, 0)], axis=0)
    h_keep = h_scr[TILE:ROWS, :]
    k_keep, v_keep = k_scr[:, TILE:ROWS, :], v_scr[:, TILE:ROWS, :]
    h_scr[0:2 * HALO, :] = h_keep
    h_scr[2 * HALO:ROWS, :] = h_new
    k_scr[:, 0:2 * HALO, :] = k_keep
    v_scr[:, 0:2 * HALO, :] = v_keep
    for r in range(0, TILE, 2 * HALO):
        store_kv(h_new[r:r + 2 * HALO], 2 * HALO + r)
    store_q(jnp.concatenate([h_keep[HALO:], h_new[:TILE - HALO]], axis=0))


def kernel(x, norm_g, w_in, attn_sink, pool_w, pool_scale, w_branch_a, w_branch_b, w_out, final_norm_g):
    batch, seq_len, d_model = x.shape
    depth = norm_g.shape[0]
    assert depth == 1, "the final RMSNorm is fused into the single layer's kernel"
    assert d_model == D_MODEL and w_in.shape[-1] == IN_WIDTH
    assert seq_len % TILE == 0 and seq_len // BLOCK >= 2 and seq_len >= 2 * POOL_EDGE_ROWS
    assert POOL_LEAD >= max(POOL_WINDOWS) // 2 and POOL_EDGE_ROWS >= max(POOL_WINDOWS) // 2
    assert TILE // BLOCK >= 2 and TILE // TAIL_ROWS >= 2
    assert ATTN_WIDTH + POOL_WIDTH == D_MODEL and len(POOL_WINDOWS) * POOL_GROUP <= D_MODEL
    bf16 = jnp.bfloat16
    bias = jnp.asarray(_attention_bias())
    inv_count = jnp.asarray(_pool_inv_count(seq_len))
    tiles_per_seq = seq_len // TILE
    n_steps = batch * tiles_per_seq
    blk_per_tile = TILE // BLOCK
    last_blk = n_steps * blk_per_tile - 1

    def resident(shape):
        return pl.BlockSpec(shape, lambda s: (0,) * len(shape), pipeline_mode=pl.Buffered(1))

    call = pl.pallas_call(
        functools.partial(_block_kernel, tiles_per_seq=tiles_per_seq),
        grid=(n_steps,),
        in_specs=[
            pl.BlockSpec((TILE, D_MODEL), lambda s: (s, 0)),
            pl.BlockSpec((TILE, D_MODEL), lambda s: (jnp.minimum(s + 1, n_steps - 1), 0)),
            pl.BlockSpec((BLOCK, D_MODEL), lambda s: (jnp.minimum((s + 2) * blk_per_tile, last_blk), 0)),
            resident((1, D_MODEL)),
            resident((D_MODEL, PACK_WIDTH)),
            resident((N_KV_HEADS, GROUP * BLOCK, 1)),
            resident((3, N_KV_HEADS, GROUP * BLOCK, 3 * BLOCK)),
            resident(inv_count.shape),
            resident((1, POOL_WIDTH)),
            resident((1, D_MODEL)),
        ],
        out_specs=pl.BlockSpec((TILE, D_MODEL), lambda s: (s, 0)),
        out_shape=jax.ShapeDtypeStruct((batch * seq_len, d_model), x.dtype),
        scratch_shapes=[
            pltpu.VMEM((ROWS, D_MODEL), bf16),
            pltpu.VMEM((N_KV_HEADS, ROWS, 2 * HEAD_DIM), bf16),
            pltpu.VMEM((N_KV_HEADS, ROWS, 2 * HEAD_DIM), bf16),
            pltpu.VMEM((TILE, ATTN_WIDTH), bf16),
            pltpu.VMEM((TILE, ATTN_WIDTH), jnp.float32),
        ],
        compiler_params=pltpu.CompilerParams(
            dimension_semantics=("arbitrary",),
            vmem_limit_bytes=VMEM_LIMIT_BYTES),
        name="hybrid_block",
    )

    x2d = x.reshape(batch * seq_len, d_model)
    sink_col = jnp.repeat(attn_sink[0].astype(jnp.float32) * LOG2_E, BLOCK).reshape(N_KV_HEADS, GROUP * BLOCK, 1)
    pool_maps = jnp.pad(pool_w[0].reshape(len(POOL_WINDOWS) * POOL_GROUP, POOL_GROUP),
                        ((0, D_MODEL - len(POOL_WINDOWS) * POOL_GROUP), (0, 0)))
    packed = jnp.concatenate([w_in[0], w_out[0], jnp.concatenate([w_branch_a[0], w_branch_b[0]], axis=0),
                              pool_maps], axis=1).astype(bf16)
    y2d = call(x2d, x2d, x2d,
               norm_g[0].reshape(1, D_MODEL), packed, sink_col, bias, inv_count,
               pool_scale[0].reshape(1, POOL_WIDTH), final_norm_g.reshape(1, D_MODEL))
    return y2d.reshape(batch, seq_len, d_model)
```

```python
import functools

import numpy as np
import jax
import jax.numpy as jnp
from jax import lax
from jax.experimental import pallas as pl
from jax.experimental.pallas import tpu as pltpu

D_MODEL = 1024
N_Q_HEADS = 8
N_KV_HEADS = 2
GROUP = N_Q_HEADS // N_KV_HEADS
HEAD_DIM = 64
ATTN_WIDTH = N_Q_HEADS * HEAD_DIM
KV_WIDTH = N_KV_HEADS * HEAD_DIM
WINDOW = 128
BLOCK = 128
POOL_WIDTH = 512
POOL_WINDOWS = (2, 4, 8, 16)
POOL_GROUP = 128
EPS = 1e-6
NEG_INF = -1e30
LOG2_E = 1.4426950408889634

OFF_Q = 0
OFF_K = OFF_Q + ATTN_WIDTH
OFF_V = OFF_K + KV_WIDTH
OFF_ZA = OFF_V + KV_WIDTH
OFF_U = OFF_ZA + ATTN_WIDTH
OFF_ZB = OFF_U + POOL_WIDTH
OFF_GA = OFF_ZB + POOL_WIDTH
OFF_GB = OFF_GA + D_MODEL
IN_WIDTH = OFF_GB + D_MODEL

TILE = 512
HALO = BLOCK
ROWS = TILE + 2 * HALO
MXU_COLS = 256
TAIL_ROWS = 256
SCORE_LEAD = 2
FILL_PER_STEP = 1
POOL_EDGE_ROWS = 16
POOL_LEAD = 16
POOL_TAIL = 32
W_IN_CHUNK = 32
W_SQ_CHUNK = 128
VMEM_LIMIT_BYTES = 56 * 1024 * 1024


def _attention_bias():
    r = np.arange(BLOCK)[:, None]
    c = np.arange(3 * BLOCK)[None, :]
    dist = np.abs(r - (c - BLOCK))
    in_window = dist <= WINDOW
    slopes = np.exp2(-8.0 * np.arange(1, N_Q_HEADS + 1, dtype=np.float64) / N_Q_HEADS)
    alibi = -slopes[:, None, None] * dist[None].astype(np.float64)
    valid = [np.ones_like(c, bool), c >= BLOCK, c < 2 * BLOCK]
    out = np.stack([np.where(in_window & v, alibi * LOG2_E, NEG_INF) for v in valid])
    return out.reshape(3, N_KV_HEADS, GROUP * BLOCK, 3 * BLOCK).astype(np.float32)


def _pool_inv_count(seq_len):
    out = np.zeros((2, 2, len(POOL_WINDOWS), POOL_EDGE_ROWS, 1), np.float64)
    for gi, w in enumerate(POOL_WINDOWS):
        half = w // 2
        out[:, 0, gi] = 1.0 / w
        for end, pos in enumerate([np.arange(POOL_EDGE_ROWS), seq_len - POOL_EDGE_ROWS + np.arange(POOL_EDGE_ROWS)]):
            count = np.minimum(pos + half, seq_len) - np.maximum(pos - half, 0)
            out[end, 1, gi, :, 0] = 1.0 / count
    return np.broadcast_to(out, out.shape[:-1] + (POOL_GROUP,)).astype(np.float32)


def _rms_norm(x, g):
    ms = jnp.mean(x * x, axis=-1, keepdims=True)
    return x * lax.rsqrt(ms + EPS) * g


def _sigmoid(x):
    return 0.5 * jnp.tanh(0.5 * x) + 0.5


def _silu(x):
    return x * _sigmoid(x)


def _cast_rows(src_hbm, dst_ref, stage_ref, sem_ref, chunk):
    n_chunks = dst_ref.shape[0] // chunk

    def copy(c):
        return pltpu.make_async_copy(src_hbm.at[pl.ds(c * chunk, chunk), :], stage_ref.at[c % 2], sem_ref.at[c % 2])

    copy(0).start()
    for c in range(n_chunks):
        if c + 1 < n_chunks:
            copy(c + 1).start()
        copy(c).wait()
        dst_ref[c * chunk:(c + 1) * chunk, :] = stage_ref[c % 2].astype(dst_ref.dtype)


def _block_kernel(x_cur_ref, x_nxt_ref, x_after_ref, norm_g_ref, w_in_hbm, sink_ref, bias_ref, cnt_ref,
                  pool_w_hbm, pool_scale_ref, w_a_hbm, w_b_hbm, w_out_hbm, final_g_ref,
                  out_ref, w_in_ref, pool_w_ref, w_a_ref, w_b_ref, w_out_ref, stage_in, stage_sq, stage_pool, sems,
                  h_scr, k_scr, v_scr, q_scr, attn_scr, *, tiles_per_seq):
    f32, bf16 = jnp.float32, jnp.bfloat16
    step = pl.program_id(0)
    i = lax.rem(step, tiles_per_seq)
    n_blk = TILE // BLOCK
    is_first = i == 0
    is_last = i == tiles_per_seq - 1
    norm_g = norm_g_ref[...]
    lane = lax.broadcasted_iota(jnp.int32, (1, 2 * HEAD_DIM), 1)
    low = lane < HEAD_DIM

    def proj(h, off, width):
        return jnp.dot(h, w_in_ref[:, off:off + width], preferred_element_type=f32)

    def proj_chunks(rows, off, width, act=lambda a: a):
        return [lambda c=c: act(proj(h_scr[rows, :], off + c, MXU_COLS)) for c in range(0, width, MXU_COLS)]

    def normed(x_ref, r):
        return _rms_norm(x_ref[r:r + BLOCK, :], norm_g).astype(bf16)

    def store_kv(h_rows, dst):
        n = h_rows.shape[0]
        kv = proj(h_rows, OFF_K, 2 * KV_WIDTH)
        for scr, a in ((k_scr, kv[:, :KV_WIDTH]), (v_scr, kv[:, KV_WIDTH:])):
            swapped = pltpu.roll(a, HEAD_DIM, axis=1)
            scr[0, dst:dst + n, :] = jnp.where(low, a, swapped).astype(bf16)
            scr[1, dst:dst + n, :] = jnp.where(low, swapped, a).astype(bf16)

    def store_q(h_rows):
        q_scr[...] = (proj(h_rows, OFF_Q, ATTN_WIDTH) * (HEAD_DIM ** -0.5 * LOG2_E)).astype(bf16)

    @pl.when(step == 0)
    def _prepare_first_tile():
        _cast_rows(w_in_hbm, w_in_ref, stage_in, sems.at[0], W_IN_CHUNK)
        _cast_rows(w_out_hbm, w_out_ref, stage_sq, sems.at[1], W_SQ_CHUNK)
        _cast_rows(w_a_hbm, w_a_ref, stage_sq, sems.at[1], W_SQ_CHUNK)
        _cast_rows(w_b_hbm, w_b_ref, stage_sq, sems.at[1], W_SQ_CHUNK)
        _cast_rows(pool_w_hbm, pool_w_ref, stage_pool, sems.at[2], W_SQ_CHUNK)
        zeros = jnp.zeros((HALO, D_MODEL), bf16)
        h_new = jnp.concatenate([normed(x_cur_ref, r) for r in range(0, TILE, BLOCK)]
                                + [normed(x_nxt_ref, 0)], axis=0)
        h_scr[0:HALO, :] = zeros
        h_scr[HALO:ROWS, :] = h_new
        store_kv(zeros, 0)
        store_kv(h_new, HALO)
        store_q(h_new[0:TILE])

    zero = jnp.zeros((), bf16)

    def scores(hk, j):
        rows = slice(j * BLOCK, (j + 1) * BLOCK)
        pieces = []
        for pair in range(GROUP // 2):
            col = (hk * GROUP // 2 + pair) * 2 * HEAD_DIM
            qp = q_scr[rows, col:col + 2 * HEAD_DIM]
            pieces.append(jnp.where(low, qp, zero))
            pieces.append(jnp.where(low, zero, qp))
        lhs = jnp.concatenate(pieces, axis=0)
        kwin = k_scr[hk, j * BLOCK:j * BLOCK + 3 * BLOCK, :]
        s = lax.dot_general(lhs, kwin, (((1,), (1,)), ((), ())), preferred_element_type=f32)
        if j == 0:
            variant = jnp.where(is_first, 1, 0)
        elif j == n_blk - 1:
            variant = jnp.where(is_last, 2, 0)
        else:
            variant = 0
        return s + bias_ref[variant, hk]

    def softmax(s, hk):
        sink_col = sink_ref[hk]
        m = jnp.maximum(jnp.max(s, axis=-1, keepdims=True), sink_col)
        p = jnp.exp2(s - m)
        denom = jnp.sum(p, axis=-1, keepdims=True) + jnp.exp2(sink_col - m)
        return p.astype(bf16), denom

    def weighted_values(p, denom, hk, j):
        rows = slice(j * BLOCK, (j + 1) * BLOCK)
        vwin = v_scr[hk, j * BLOCK:j * BLOCK + 3 * BLOCK, :]
        o = jnp.dot(p, vwin, preferred_element_type=f32) / denom
        for pair in range(GROUP // 2):
            even = o[(2 * pair) * BLOCK:(2 * pair + 1) * BLOCK]
            odd = o[(2 * pair + 1) * BLOCK:(2 * pair + 2) * BLOCK]
            col = (hk * GROUP // 2 + pair) * 2 * HEAD_DIM
            attn_scr[rows, col:col + 2 * HEAD_DIM] = jnp.where(low, even, odd)

    def pooled_group(u, gi):
        w = POOL_WINDOWS[gi]
        r0, n = POOL_LEAD, TILE
        if w == 2:
            wsum = u[r0 - 1:r0 - 1 + n] + u[r0:r0 + n]
        else:
            c = u[0:n + 32] + u[1:n + 33]
            if w == 4:
                wsum = c[r0 - 2:r0 - 2 + n] + c[r0:r0 + n]
            else:
                e = c[0:n + 24] + c[2:n + 26]
                if w == 8:
                    wsum = e[r0 - 4:r0 - 4 + n] + e[r0:r0 + n]
                else:
                    f = e[0:n + 16] + e[4:n + 20]
                    wsum = f[r0 - 8:r0 - 8 + n] + f[r0:r0 + n]
        edge = POOL_EDGE_ROWS
        head = wsum[:edge] * cnt_ref[0, jnp.where(is_first, 1, 0), gi]
        body = wsum[edge:n - edge] * (1.0 / w)
        tail = wsum[n - edge:] * cnt_ref[1, jnp.where(is_last, 1, 0), gi]
        pooled = jnp.concatenate([head, body, tail], axis=0) - u[r0:r0 + n]
        pool_w = pool_w_ref[gi * POOL_GROUP:(gi + 1) * POOL_GROUP, :]
        return jnp.dot(pooled.astype(bf16), pool_w, preferred_element_type=f32)

    centre = slice(HALO, HALO + TILE)
    pool_rows = slice(HALO - POOL_LEAD, HALO + TILE + POOL_TAIL)
    work = [(hk, j) for j in range(n_blk) for hk in range(N_KV_HEADS)]
    fillers = (proj_chunks(centre, OFF_GA, D_MODEL, _sigmoid)
               + proj_chunks(centre, OFF_GB, D_MODEL, _sigmoid)
               + proj_chunks(centre, OFF_ZA, ATTN_WIDTH)
               + proj_chunks(pool_rows, OFF_U, POOL_WIDTH)
               + proj_chunks(centre, OFF_ZB, POOL_WIDTH))
    filled = []
    s_list = [scores(*work[n]) for n in range(SCORE_LEAD)]
    p_prev = None
    for n, (hk, j) in enumerate(work):
        filled += [f() for f in fillers[FILL_PER_STEP * n:FILL_PER_STEP * (n + 1)]]
        if n + SCORE_LEAD < len(work):
            s_list.append(scores(*work[n + SCORE_LEAD]))
        p_cur = softmax(s_list[n], hk)
        if p_prev is not None:
            weighted_values(*p_prev, *work[n - 1])
        p_prev = p_cur
    filled += [f() for f in fillers[FILL_PER_STEP * len(work):]]
    weighted_values(*p_prev, *work[-1])

    def take(width):
        chunks = [filled.pop(0) for _ in range(width // MXU_COLS)]
        return jnp.concatenate(chunks, axis=1)

    gate_a = take(D_MODEL)
    gate_b = take(D_MODEL)
    z_a = take(ATTN_WIDTH)
    u_all = take(POOL_WIDTH)
    z_b = take(POOL_WIDTH)
    u_all = jnp.concatenate([jnp.where(is_first, 0.0, u_all[:POOL_LEAD]),
                             u_all[POOL_LEAD:POOL_LEAD + TILE],
                             jnp.where(is_last, 0.0, u_all[POOL_LEAD + TILE:])], axis=0)

    gated_a = (attn_scr[...] * _silu(z_a)).astype(bf16)
    y_a = jnp.dot(gated_a, w_a_ref[...], preferred_element_type=f32)
    mixed = [pooled_group(u_all[:, gi * POOL_GROUP:(gi + 1) * POOL_GROUP], gi)
             for gi in range(len(POOL_WINDOWS))]
    mixed = jnp.concatenate(mixed, axis=1) * pool_scale_ref[...]
    gated_b = (mixed * _silu(z_b)).astype(bf16)
    y_b = jnp.dot(gated_b, w_b_ref[...], preferred_element_type=f32)
    t_a = gate_a * y_a

    final_g = final_g_ref[...]
    for r in range(0, TILE, TAIL_ROWS):
        rows = slice(r, r + TAIL_ROWS)
        merged = (t_a[rows] + gate_b[rows] * y_b[rows]).astype(bf16)
        y = x_cur_ref[rows, :] + jnp.dot(merged, w_out_ref[...], preferred_element_type=f32)
        out_ref[rows, :] = _rms_norm(y, final_g)

    h_new = jnp.concatenate([normed(x_nxt_ref, r) for r in range(HALO, TILE, BLOCK)]
                            + [normed(x_after_ref, 0)], axis=0)
    h_keep = h_scr[TILE:ROWS, :]
    k_keep, v_keep = k_scr[:, TILE:ROWS, :], v_scr[:, TILE:ROWS, :]
    h_scr[0:2 * HALO, :] = h_keep
    h_scr[2 * HALO:ROWS, :] = h_new
    k_scr[:, 0:2 * HALO, :] = k_keep
    v_scr[:, 0:2 * HALO, :] = v_keep
    for r in range(0, TILE, 2 * HALO):
        store_kv(h_new[r:r + 2 * HALO], 2 * HALO + r)
    store_q(jnp.concatenate([h_keep[HALO:], h_new[:TILE - HALO]], axis=0))


def kernel(x, norm_g, w_in, attn_sink, pool_w, pool_scale, w_branch_a, w_branch_b, w_out, final_norm_g):
    batch, seq_len, d_model = x.shape
    depth = norm_g.shape[0]
    assert depth == 1, "the final RMSNorm is fused into the single layer's kernel"
    assert d_model == D_MODEL and w_in.shape[-1] == IN_WIDTH
    assert seq_len % TILE == 0 and seq_len // BLOCK >= 2 and seq_len >= 2 * POOL_EDGE_ROWS
    assert POOL_LEAD >= max(POOL_WINDOWS) // 2 and POOL_EDGE_ROWS >= max(POOL_WINDOWS) // 2
    assert TILE // BLOCK >= 2 and TILE // TAIL_ROWS >= 2
    bf16 = jnp.bfloat16
    bias = jnp.asarray(_attention_bias())
    inv_count = jnp.asarray(_pool_inv_count(seq_len))
    tiles_per_seq = seq_len // TILE
    n_steps = batch * tiles_per_seq
    blk_per_tile = TILE // BLOCK
    last_blk = n_steps * blk_per_tile - 1
    pool_rows = len(POOL_WINDOWS) * POOL_GROUP

    def resident(shape):
        return pl.BlockSpec(shape, lambda s: (0,) * len(shape), pipeline_mode=pl.Buffered(1))

    in_hbm = pl.BlockSpec(memory_space=pl.ANY)
    call = pl.pallas_call(
        functools.partial(_block_kernel, tiles_per_seq=tiles_per_seq),
        grid=(n_steps,),
        in_specs=[
            pl.BlockSpec((TILE, D_MODEL), lambda s: (s, 0)),
            pl.BlockSpec((TILE, D_MODEL), lambda s: (jnp.minimum(s + 1, n_steps - 1), 0)),
            pl.BlockSpec((BLOCK, D_MODEL), lambda s: (jnp.minimum((s + 2) * blk_per_tile, last_blk), 0)),
            resident((1, D_MODEL)),
            in_hbm,
            resident((N_KV_HEADS, GROUP * BLOCK, 1)),
            resident((3, N_KV_HEADS, GROUP * BLOCK, 3 * BLOCK)),
            resident(inv_count.shape),
            in_hbm,
            resident((1, POOL_WIDTH)),
            in_hbm,
            in_hbm,
            in_hbm,
            resident((1, D_MODEL)),
        ],
        out_specs=pl.BlockSpec((TILE, D_MODEL), lambda s: (s, 0)),
        out_shape=jax.ShapeDtypeStruct((batch * seq_len, d_model), x.dtype),
        scratch_shapes=[
            pltpu.VMEM((D_MODEL, IN_WIDTH), bf16),
            pltpu.VMEM((pool_rows, POOL_GROUP), bf16),
            pltpu.VMEM((ATTN_WIDTH, D_MODEL), bf16),
            pltpu.VMEM((POOL_WIDTH, D_MODEL), bf16),
            pltpu.VMEM((D_MODEL, D_MODEL), bf16),
            pltpu.VMEM((2, W_IN_CHUNK, IN_WIDTH), jnp.float32),
            pltpu.VMEM((2, W_SQ_CHUNK, D_MODEL), jnp.float32),
            pltpu.VMEM((2, W_SQ_CHUNK, POOL_GROUP), jnp.float32),
            pltpu.SemaphoreType.DMA((3, 2)),
            pltpu.VMEM((ROWS, D_MODEL), bf16),
            pltpu.VMEM((N_KV_HEADS, ROWS, 2 * HEAD_DIM), bf16),
            pltpu.VMEM((N_KV_HEADS, ROWS, 2 * HEAD_DIM), bf16),
            pltpu.VMEM((TILE, ATTN_WIDTH), bf16),
            pltpu.VMEM((TILE, ATTN_WIDTH), jnp.float32),
        ],
        compiler_params=pltpu.CompilerParams(
            dimension_semantics=("arbitrary",),
            vmem_limit_bytes=VMEM_LIMIT_BYTES),
        name="hybrid_block",
    )

    x2d = x.reshape(batch * seq_len, d_model)
    sink_col = jnp.repeat(attn_sink[0].astype(jnp.float32) * LOG2_E, BLOCK).reshape(N_KV_HEADS, GROUP * BLOCK, 1)
    y2d = call(x2d, x2d, x2d,
               norm_g[0].reshape(1, D_MODEL), w_in[0], sink_col, bias, inv_count,
               pool_w[0].reshape(pool_rows, POOL_GROUP), pool_scale[0].reshape(1, POOL_WIDTH),
               w_branch_a[0], w_branch_b[0], w_out[0],
               final_norm_g.reshape(1, D_MODEL))
    return y2d.reshape(batch, seq_len, d_model)
```

```python
import functools

import numpy as np
import jax
import jax.numpy as jnp
from jax import lax
from jax.experimental import pallas as pl
from jax.experimental.pallas import tpu as pltpu

D_MODEL = 1024
N_Q_HEADS = 8
N_KV_HEADS = 2
GROUP = N_Q_HEADS // N_KV_HEADS
HEAD_DIM = 64
ATTN_WIDTH = N_Q_HEADS * HEAD_DIM
KV_WIDTH = N_KV_HEADS * HEAD_DIM
WINDOW = 128
BLOCK = 128
POOL_WIDTH = 512
POOL_WINDOWS = (2, 4, 8, 16)
POOL_GROUP = 128
EPS = 1e-6
NEG_INF = -1e30
LOG2_E = 1.4426950408889634

OFF_Q = 0
OFF_K = OFF_Q + ATTN_WIDTH
OFF_V = OFF_K + KV_WIDTH
OFF_ZA = OFF_V + KV_WIDTH
OFF_U = OFF_ZA + ATTN_WIDTH
OFF_ZB = OFF_U + POOL_WIDTH
OFF_GA = OFF_ZB + POOL_WIDTH
OFF_GB = OFF_GA + D_MODEL
IN_WIDTH = OFF_GB + D_MODEL

TILE = 512
HALO = BLOCK
ROWS = TILE + 2 * HALO
MXU_COLS = 256
TAIL_ROWS = 256
SCORE_LEAD = 2
FILL_PER_STEP = 1
POOL_EDGE_ROWS = 16
POOL_LEAD = 16
POOL_TAIL = 32
W_IN_CHUNK = 64
W_SQ_CHUNK = 128
W_SLOTS = 3
VMEM_LIMIT_BYTES = 56 * 1024 * 1024


def _attention_bias():
    r = np.arange(BLOCK)[:, None]
    c = np.arange(3 * BLOCK)[None, :]
    dist = np.abs(r - (c - BLOCK))
    in_window = dist <= WINDOW
    slopes = np.exp2(-8.0 * np.arange(1, N_Q_HEADS + 1, dtype=np.float64) / N_Q_HEADS)
    alibi = -slopes[:, None, None] * dist[None].astype(np.float64)
    valid = [np.ones_like(c, bool), c >= BLOCK, c < 2 * BLOCK]
    out = np.stack([np.where(in_window & v, alibi * LOG2_E, NEG_INF) for v in valid])
    return out.reshape(3, N_KV_HEADS, GROUP * BLOCK, 3 * BLOCK).astype(np.float32)


def _pool_inv_count(seq_len):
    out = np.zeros((2, 2, len(POOL_WINDOWS), POOL_EDGE_ROWS, 1), np.float64)
    for gi, w in enumerate(POOL_WINDOWS):
        half = w // 2
        out[:, 0, gi] = 1.0 / w
        for end, pos in enumerate([np.arange(POOL_EDGE_ROWS), seq_len - POOL_EDGE_ROWS + np.arange(POOL_EDGE_ROWS)]):
            count = np.minimum(pos + half, seq_len) - np.maximum(pos - half, 0)
            out[end, 1, gi, :, 0] = 1.0 / count
    return np.broadcast_to(out, out.shape[:-1] + (POOL_GROUP,)).astype(np.float32)


def _rms_norm(x, g):
    ms = jnp.mean(x * x, axis=-1, keepdims=True)
    return x * lax.rsqrt(ms + EPS) * g


def _sigmoid(x):
    return 0.5 * jnp.tanh(0.5 * x) + 0.5


def _silu(x):
    return x * _sigmoid(x)


def _cast_rows(src_hbm, dst_ref, stage_ref, sem_ref):
    n_slots, chunk = stage_ref.shape[0], stage_ref.shape[1]
    n_chunks = dst_ref.shape[0] // chunk

    def copy(c):
        slot = c % n_slots
        return pltpu.make_async_copy(src_hbm.at[pl.ds(c * chunk, chunk), :], stage_ref.at[slot], sem_ref.at[slot])

    for c in range(min(n_slots, n_chunks)):
        copy(c).start()
    for c in range(n_chunks):
        copy(c).wait()
        dst_ref[c * chunk:(c + 1) * chunk, :] = stage_ref[c % n_slots].astype(dst_ref.dtype)
        if c + n_slots < n_chunks:
            copy(c + n_slots).start()


def _block_kernel(x_cur_ref, x_nxt_ref, x_after_ref, norm_g_ref, w_in_hbm, sink_ref, bias_ref, cnt_ref,
                  pool_w_hbm, pool_scale_ref, w_a_hbm, w_b_hbm, w_out_hbm, final_g_ref,
                  out_ref, w_in_ref, pool_w_ref, w_a_ref, w_b_ref, w_out_ref, stage_in, stage_sq, stage_pool, sems,
                  h_scr, k_scr, v_scr, q_scr, attn_scr, *, tiles_per_seq):
    f32, bf16 = jnp.float32, jnp.bfloat16
    step = pl.program_id(0)
    i = lax.rem(step, tiles_per_seq)
    n_blk = TILE // BLOCK
    is_first = i == 0
    is_last = i == tiles_per_seq - 1
    norm_g = norm_g_ref[...]
    lane = lax.broadcasted_iota(jnp.int32, (1, 2 * HEAD_DIM), 1)
    low = lane < HEAD_DIM

    def proj(h, off, width):
        return jnp.dot(h, w_in_ref[:, off:off + width], preferred_element_type=f32)

    def proj_chunks(rows, off, width, act=lambda a: a):
        return [lambda c=c: act(proj(h_scr[rows, :], off + c, MXU_COLS)) for c in range(0, width, MXU_COLS)]

    def normed(x_ref, r):
        return _rms_norm(x_ref[r:r + BLOCK, :], norm_g).astype(bf16)

    def store_kv(h_rows, dst):
        n = h_rows.shape[0]
        kv = proj(h_rows, OFF_K, 2 * KV_WIDTH)
        for scr, a in ((k_scr, kv[:, :KV_WIDTH]), (v_scr, kv[:, KV_WIDTH:])):
            swapped = pltpu.roll(a, HEAD_DIM, axis=1)
            scr[0, dst:dst + n, :] = jnp.where(low, a, swapped).astype(bf16)
            scr[1, dst:dst + n, :] = jnp.where(low, swapped, a).astype(bf16)

    def store_q(h_rows):
        q_scr[...] = (proj(h_rows, OFF_Q, ATTN_WIDTH) * (HEAD_DIM ** -0.5 * LOG2_E)).astype(bf16)

    @pl.when(step == 0)
    def _prepare_first_tile():
        _cast_rows(w_in_hbm, w_in_ref, stage_in, sems.at[0])
        _cast_rows(w_out_hbm, w_out_ref, stage_sq, sems.at[1])
        _cast_rows(w_a_hbm, w_a_ref, stage_sq, sems.at[1])
        _cast_rows(w_b_hbm, w_b_ref, stage_sq, sems.at[1])
        _cast_rows(pool_w_hbm, pool_w_ref, stage_pool, sems.at[2])
        zeros = jnp.zeros((HALO, D_MODEL), bf16)
        h_new = jnp.concatenate([normed(x_cur_ref, r) for r in range(0, TILE, BLOCK)]
                                + [normed(x_nxt_ref, 0)], axis=0)
        h_scr[0:HALO, :] = zeros
        h_scr[HALO:ROWS, :] = h_new
        store_kv(zeros, 0)
        store_kv(h_new, HALO)
        store_q(h_new[0:TILE])

    zero = jnp.zeros((), bf16)

    def scores(hk, j):
        rows = slice(j * BLOCK, (j + 1) * BLOCK)
        pieces = []
        for pair in range(GROUP // 2):
            col = (hk * GROUP // 2 + pair) * 2 * HEAD_DIM
            qp = q_scr[rows, col:col + 2 * HEAD_DIM]
            pieces.append(jnp.where(low, qp, zero))
            pieces.append(jnp.where(low, zero, qp))
        lhs = jnp.concatenate(pieces, axis=0)
        kwin = k_scr[hk, j * BLOCK:j * BLOCK + 3 * BLOCK, :]
        s = lax.dot_general(lhs, kwin, (((1,), (1,)), ((), ())), preferred_element_type=f32)
        if j == 0:
            variant = jnp.where(is_first, 1, 0)
        elif j == n_blk - 1:
            variant = jnp.where(is_last, 2, 0)
        else:
            variant = 0
        return s + bias_ref[variant, hk]

    def softmax(s, hk):
        sink_col = sink_ref[hk]
        m = jnp.maximum(jnp.max(s, axis=-1, keepdims=True), sink_col)
        p = jnp.exp2(s - m)
        denom = jnp.sum(p, axis=-1, keepdims=True) + jnp.exp2(sink_col - m)
        return p.astype(bf16), denom

    def weighted_values(p, denom, hk, j):
        rows = slice(j * BLOCK, (j + 1) * BLOCK)
        vwin = v_scr[hk, j * BLOCK:j * BLOCK + 3 * BLOCK, :]
        o = jnp.dot(p, vwin, preferred_element_type=f32) / denom
        for pair in range(GROUP // 2):
            even = o[(2 * pair) * BLOCK:(2 * pair + 1) * BLOCK]
            odd = o[(2 * pair + 1) * BLOCK:(2 * pair + 2) * BLOCK]
            col = (hk * GROUP // 2 + pair) * 2 * HEAD_DIM
            attn_scr[rows, col:col + 2 * HEAD_DIM] = jnp.where(low, even, odd)

    def pooled_group(u, gi):
        w = POOL_WINDOWS[gi]
        r0, n = POOL_LEAD, TILE
        if w == 2:
            wsum = u[r0 - 1:r0 - 1 + n] + u[r0:r0 + n]
        else:
            c = u[0:n + 32] + u[1:n + 33]
            if w == 4:
                wsum = c[r0 - 2:r0 - 2 + n] + c[r0:r0 + n]
            else:
                e = c[0:n + 24] + c[2:n + 26]
                if w == 8:
                    wsum = e[r0 - 4:r0 - 4 + n] + e[r0:r0 + n]
                else:
                    f = e[0:n + 16] + e[4:n + 20]
                    wsum = f[r0 - 8:r0 - 8 + n] + f[r0:r0 + n]
        edge = POOL_EDGE_ROWS
        head = wsum[:edge] * cnt_ref[0, jnp.where(is_first, 1, 0), gi]
        body = wsum[edge:n - edge] * (1.0 / w)
        tail = wsum[n - edge:] * cnt_ref[1, jnp.where(is_last, 1, 0), gi]
        pooled = jnp.concatenate([head, body, tail], axis=0) - u[r0:r0 + n]
        pool_w = pool_w_ref[gi * POOL_GROUP:(gi + 1) * POOL_GROUP, :]
        return jnp.dot(pooled.astype(bf16), pool_w, preferred_element_type=f32)

    centre = slice(HALO, HALO + TILE)
    pool_rows = slice(HALO - POOL_LEAD, HALO + TILE + POOL_TAIL)
    work = [(hk, j) for j in range(n_blk) for hk in range(N_KV_HEADS)]
    fillers = (proj_chunks(centre, OFF_GA, D_MODEL, _sigmoid)
               + proj_chunks(centre, OFF_GB, D_MODEL, _sigmoid)
               + proj_chunks(centre, OFF_ZA, ATTN_WIDTH)
               + proj_chunks(pool_rows, OFF_U, POOL_WIDTH)
               + proj_chunks(centre, OFF_ZB, POOL_WIDTH))
    filled = []
    s_list = [scores(*work[n]) for n in range(SCORE_LEAD)]
    p_prev = None
    for n, (hk, j) in enumerate(work):
        filled += [f() for f in fillers[FILL_PER_STEP * n:FILL_PER_STEP * (n + 1)]]
        if n + SCORE_LEAD < len(work):
            s_list.append(scores(*work[n + SCORE_LEAD]))
        p_cur = softmax(s_list[n], hk)
        if p_prev is not None:
            weighted_values(*p_prev, *work[n - 1])
        p_prev = p_cur
    filled += [f() for f in fillers[FILL_PER_STEP * len(work):]]
    weighted_values(*p_prev, *work[-1])

    def take(width):
        chunks = [filled.pop(0) for _ in range(width // MXU_COLS)]
        return jnp.concatenate(chunks, axis=1)

    gate_a = take(D_MODEL)
    gate_b = take(D_MODEL)
    z_a = take(ATTN_WIDTH)
    u_all = take(POOL_WIDTH)
    z_b = take(POOL_WIDTH)
    u_all = jnp.concatenate([jnp.where(is_first, 0.0, u_all[:POOL_LEAD]),
                             u_all[POOL_LEAD:POOL_LEAD + TILE],
                             jnp.where(is_last, 0.0, u_all[POOL_LEAD + TILE:])], axis=0)

    gated_a = (attn_scr[...] * _silu(z_a)).astype(bf16)
    y_a = jnp.dot(gated_a, w_a_ref[...], preferred_element_type=f32)
    mixed = [pooled_group(u_all[:, gi * POOL_GROUP:(gi + 1) * POOL_GROUP], gi)
             for gi in range(len(POOL_WINDOWS))]
    mixed = jnp.concatenate(mixed, axis=1) * pool_scale_ref[...]
    gated_b = (mixed * _silu(z_b)).astype(bf16)
    y_b = jnp.dot(gated_b, w_b_ref[...], preferred_element_type=f32)
    t_a = gate_a * y_a

    final_g = final_g_ref[...]
    for r in range(0, TILE, TAIL_ROWS):
        rows = slice(r, r + TAIL_ROWS)
        merged = (t_a[rows] + gate_b[rows] * y_b[rows]).astype(bf16)
        y = x_cur_ref[rows, :] + jnp.dot(merged, w_out_ref[...], preferred_element_type=f32)
        out_ref[rows, :] = _rms_norm(y, final_g)

    h_new = jnp.concatenate([normed(x_nxt_ref, r) for r in range(HALO, TILE, BLOCK)]
                            + [normed(x_after_ref, 0)], axis=0)
    h_keep = h_scr[TILE:ROWS, :]
    k_keep, v_keep = k_scr[:, TILE:ROWS, :], v_scr[:, TILE:ROWS, :]
    h_scr[0:2 * HALO, :] = h_keep
    h_scr[2 * HALO:ROWS, :] = h_new
    k_scr[:, 0:2 * HALO, :] = k_keep
    v_scr[:, 0:2 * HALO, :] = v_keep
    for r in range(0, TILE, 2 * HALO):
        store_kv(h_new[r:r + 2 * HALO], 2 * HALO + r)
    store_q(jnp.concatenate([h_keep[HALO:], h_new[:TILE - HALO]], axis=0))


def kernel(x, norm_g, w_in, attn_sink, pool_w, pool_scale, w_branch_a, w_branch_b, w_out, final_norm_g):
    batch, seq_len, d_model = x.shape
    depth = norm_g.shape[0]
    assert depth == 1, "the final RMSNorm is fused into the single layer's kernel"
    assert d_model == D_MODEL and w_in.shape[-1] == IN_WIDTH
    assert seq_len % TILE == 0 and seq_len // BLOCK >= 2 and seq_len >= 2 * POOL_EDGE_ROWS
    assert POOL_LEAD >= max(POOL_WINDOWS) // 2 and POOL_EDGE_ROWS >= max(POOL_WINDOWS) // 2
    assert TILE // BLOCK >= 2 and TILE // TAIL_ROWS >= 2
    bf16 = jnp.bfloat16
    bias = jnp.asarray(_attention_bias())
    inv_count = jnp.asarray(_pool_inv_count(seq_len))
    tiles_per_seq = seq_len // TILE
    n_steps = batch * tiles_per_seq
    blk_per_tile = TILE // BLOCK
    last_blk = n_steps * blk_per_tile - 1
    pool_rows = len(POOL_WINDOWS) * POOL_GROUP

    def resident(shape):
        return pl.BlockSpec(shape, lambda s: (0,) * len(shape), pipeline_mode=pl.Buffered(1))

    in_hbm = pl.BlockSpec(memory_space=pl.ANY)
    call = pl.pallas_call(
        functools.partial(_block_kernel, tiles_per_seq=tiles_per_seq),
        grid=(n_steps,),
        in_specs=[
            pl.BlockSpec((TILE, D_MODEL), lambda s: (s, 0)),
            pl.BlockSpec((TILE, D_MODEL), lambda s: (jnp.minimum(s + 1, n_steps - 1), 0)),
            pl.BlockSpec((BLOCK, D_MODEL), lambda s: (jnp.minimum((s + 2) * blk_per_tile, last_blk), 0)),
            resident((1, D_MODEL)),
            in_hbm,
            resident((N_KV_HEADS, GROUP * BLOCK, 1)),
            resident((3, N_KV_HEADS, GROUP * BLOCK, 3 * BLOCK)),
            resident(inv_count.shape),
            in_hbm,
            resident((1, POOL_WIDTH)),
            in_hbm,
            in_hbm,
            in_hbm,
            resident((1, D_MODEL)),
        ],
        out_specs=pl.BlockSpec((TILE, D_MODEL), lambda s: (s, 0)),
        out_shape=jax.ShapeDtypeStruct((batch * seq_len, d_model), x.dtype),
        scratch_shapes=[
            pltpu.VMEM((D_MODEL, IN_WIDTH), bf16),
            pltpu.VMEM((pool_rows, POOL_GROUP), bf16),
            pltpu.VMEM((ATTN_WIDTH, D_MODEL), bf16),
            pltpu.VMEM((POOL_WIDTH, D_MODEL), bf16),
            pltpu.VMEM((D_MODEL, D_MODEL), bf16),
            pltpu.VMEM((W_SLOTS, W_IN_CHUNK, IN_WIDTH), jnp.float32),
            pltpu.VMEM((W_SLOTS, W_SQ_CHUNK, D_MODEL), jnp.float32),
            pltpu.VMEM((W_SLOTS, W_SQ_CHUNK, POOL_GROUP), jnp.float32),
            pltpu.SemaphoreType.DMA((3, W_SLOTS)),
            pltpu.VMEM((ROWS, D_MODEL), bf16),
            pltpu.VMEM((N_KV_HEADS, ROWS, 2 * HEAD_DIM), bf16),
            pltpu.VMEM((N_KV_HEADS, ROWS, 2 * HEAD_DIM), bf16),
            pltpu.VMEM((TILE, ATTN_WIDTH), bf16),
            pltpu.VMEM((TILE, ATTN_WIDTH), jnp.float32),
        ],
        compiler_params=pltpu.CompilerParams(
            dimension_semantics=("arbitrary",),
            vmem_limit_bytes=VMEM_LIMIT_BYTES),
        name="hybrid_block",
    )

    x2d = x.reshape(batch * seq_len, d_model)
    sink_col = jnp.repeat(attn_sink[0].astype(jnp.float32) * LOG2_E, BLOCK).reshape(N_KV_HEADS, GROUP * BLOCK, 1)
    y2d = call(x2d, x2d, x2d,
               norm_g[0].reshape(1, D_MODEL), w_in[0], sink_col, bias, inv_count,
               pool_w[0].reshape(pool_rows, POOL_GROUP), pool_scale[0].reshape(1, POOL_WIDTH),
               w_branch_a[0], w_branch_b[0], w_out[0],
               final_norm_g.reshape(1, D_MODEL))
    return y2d.reshape(batch, seq_len, d_model)
```

```python
import functools

import numpy as np
import jax
import jax.numpy as jnp
from jax import lax
from jax.experimental import pallas as pl
from jax.experimental.pallas import tpu as pltpu

D_MODEL = 1024
N_Q_HEADS = 8
N_KV_HEADS = 2
GROUP = N_Q_HEADS // N_KV_HEADS
HEAD_DIM = 64
ATTN_WIDTH = N_Q_HEADS * HEAD_DIM
KV_WIDTH = N_KV_HEADS * HEAD_DIM
WINDOW = 128
BLOCK = 128
POOL_WIDTH = 512
POOL_WINDOWS = (2, 4, 8, 16)
POOL_GROUP = 128
EPS = 1e-6
NEG_INF = -1e30
LOG2_E = 1.4426950408889634

OFF_Q = 0
OFF_K = OFF_Q + ATTN_WIDTH
OFF_V = OFF_K + KV_WIDTH
OFF_ZA = OFF_V + KV_WIDTH
OFF_U = OFF_ZA + ATTN_WIDTH
OFF_ZB = OFF_U + POOL_WIDTH
OFF_GA = OFF_ZB + POOL_WIDTH
OFF_GB = OFF_GA + D_MODEL
IN_WIDTH = OFF_GB + D_MODEL

TILE = 512
HALO = BLOCK
ROWS = TILE + 2 * HALO
MXU_COLS = 256
TAIL_ROWS = 256
SCORE_LEAD = 2
FILL_PER_STEP = 1
POOL_EDGE_ROWS = 16
POOL_LEAD = 16
POOL_TAIL = 32
W_IN_CHUNK = 64
W_SQ_CHUNK = 128
W_SLOTS = 3
VMEM_LIMIT_BYTES = 56 * 1024 * 1024


def _attention_bias():
    r = np.arange(BLOCK)[:, None]
    c = np.arange(3 * BLOCK)[None, :]
    dist = np.abs(r - (c - BLOCK))
    in_window = dist <= WINDOW
    slopes = np.exp2(-8.0 * np.arange(1, N_Q_HEADS + 1, dtype=np.float64) / N_Q_HEADS)
    alibi = -slopes[:, None, None] * dist[None].astype(np.float64)
    valid = [np.ones_like(c, bool), c >= BLOCK, c < 2 * BLOCK]
    out = np.stack([np.where(in_window & v, alibi * LOG2_E, NEG_INF) for v in valid])
    return out.reshape(3, N_KV_HEADS, GROUP * BLOCK, 3 * BLOCK).astype(np.float32)


def _pool_inv_count(seq_len):
    out = np.zeros((2, 2, len(POOL_WINDOWS), POOL_EDGE_ROWS, 1), np.float64)
    for gi, w in enumerate(POOL_WINDOWS):
        half = w // 2
        out[:, 0, gi] = 1.0 / w
        for end, pos in enumerate([np.arange(POOL_EDGE_ROWS), seq_len - POOL_EDGE_ROWS + np.arange(POOL_EDGE_ROWS)]):
            count = np.minimum(pos + half, seq_len) - np.maximum(pos - half, 0)
            out[end, 1, gi, :, 0] = 1.0 / count
    return np.broadcast_to(out, out.shape[:-1] + (POOL_GROUP,)).astype(np.float32)


def _rms_norm(x, g):
    ms = jnp.mean(x * x, axis=-1, keepdims=True)
    return x * lax.rsqrt(ms + EPS) * g


def _sigmoid(x):
    return 0.5 * jnp.tanh(0.5 * x) + 0.5


def _silu(x):
    return x * _sigmoid(x)


def _cast_weights(streams):
    rings = []
    for pairs, stage_ref, sem_ref in streams:
        chunk = stage_ref.shape[1]
        jobs = [(src, dst, r) for src, dst in pairs for r in range(0, dst.shape[0], chunk)]
        rings.append((jobs, stage_ref, sem_ref))

    def copy(ring, c):
        jobs, stage_ref, sem_ref = ring
        src, _, r = jobs[c]
        slot = c % stage_ref.shape[0]
        return pltpu.make_async_copy(src.at[pl.ds(r, stage_ref.shape[1]), :], stage_ref.at[slot], sem_ref.at[slot])

    for ring in rings:
        for c in range(min(ring[1].shape[0], len(ring[0]))):
            copy(ring, c).start()
    for c in range(max(len(ring[0]) for ring in rings)):
        for ring in rings:
            jobs, stage_ref, _ = ring
            if c < len(jobs):
                n_slots, chunk = stage_ref.shape[0], stage_ref.shape[1]
                _, dst, r = jobs[c]
                copy(ring, c).wait()
                dst[r:r + chunk, :] = stage_ref[c % n_slots].astype(dst.dtype)
                if c + n_slots < len(jobs):
                    copy(ring, c + n_slots).start()


def _block_kernel(x_cur_ref, x_nxt_ref, x_after_ref, norm_g_ref, w_in_hbm, sink_ref, bias_ref, cnt_ref,
                  pool_w_hbm, pool_scale_ref, w_a_hbm, w_b_hbm, w_out_hbm, final_g_ref,
                  out_ref, w_in_ref, pool_w_ref, w_a_ref, w_b_ref, w_out_ref, stage_in, stage_sq, stage_pool, sems,
                  h_scr, k_scr, v_scr, q_scr, attn_scr, *, tiles_per_seq):
    f32, bf16 = jnp.float32, jnp.bfloat16
    step = pl.program_id(0)
    i = lax.rem(step, tiles_per_seq)
    n_blk = TILE // BLOCK
    is_first = i == 0
    is_last = i == tiles_per_seq - 1
    norm_g = norm_g_ref[...]
    lane = lax.broadcasted_iota(jnp.int32, (1, 2 * HEAD_DIM), 1)
    low = lane < HEAD_DIM

    def proj(h, off, width):
        return jnp.dot(h, w_in_ref[:, off:off + width], preferred_element_type=f32)

    def proj_chunks(rows, off, width, act=lambda a: a):
        return [lambda c=c: act(proj(h_scr[rows, :], off + c, MXU_COLS)) for c in range(0, width, MXU_COLS)]

    def normed(x_ref, r):
        return _rms_norm(x_ref[r:r + BLOCK, :], norm_g).astype(bf16)

    def store_kv(h_rows, dst):
        n = h_rows.shape[0]
        kv = proj(h_rows, OFF_K, 2 * KV_WIDTH)
        for scr, a in ((k_scr, kv[:, :KV_WIDTH]), (v_scr, kv[:, KV_WIDTH:])):
            swapped = pltpu.roll(a, HEAD_DIM, axis=1)
            scr[0, dst:dst + n, :] = jnp.where(low, a, swapped).astype(bf16)
            scr[1, dst:dst + n, :] = jnp.where(low, swapped, a).astype(bf16)

    def store_q(h_rows):
        q_scr[...] = (proj(h_rows, OFF_Q, ATTN_WIDTH) * (HEAD_DIM ** -0.5 * LOG2_E)).astype(bf16)

    @pl.when(step == 0)
    def _prepare_first_tile():
        _cast_weights([
            ([(w_in_hbm, w_in_ref)], stage_in, sems.at[0]),
            ([(w_out_hbm, w_out_ref), (w_a_hbm, w_a_ref), (w_b_hbm, w_b_ref)], stage_sq, sems.at[1]),
            ([(pool_w_hbm, pool_w_ref)], stage_pool, sems.at[2]),
        ])
        zeros = jnp.zeros((HALO, D_MODEL), bf16)
        h_new = jnp.concatenate([normed(x_cur_ref, r) for r in range(0, TILE, BLOCK)]
                                + [normed(x_nxt_ref, 0)], axis=0)
        h_scr[0:HALO, :] = zeros
        h_scr[HALO:ROWS, :] = h_new
        store_kv(zeros, 0)
        store_kv(h_new, HALO)
        store_q(h_new[0:TILE])

    zero = jnp.zeros((), bf16)

    def scores(hk, j):
        rows = slice(j * BLOCK, (j + 1) * BLOCK)
        pieces = []
        for pair in range(GROUP // 2):
            col = (hk * GROUP // 2 + pair) * 2 * HEAD_DIM
            qp = q_scr[rows, col:col + 2 * HEAD_DIM]
            pieces.append(jnp.where(low, qp, zero))
            pieces.append(jnp.where(low, zero, qp))
        lhs = jnp.concatenate(pieces, axis=0)
        kwin = k_scr[hk, j * BLOCK:j * BLOCK + 3 * BLOCK, :]
        s = lax.dot_general(lhs, kwin, (((1,), (1,)), ((), ())), preferred_element_type=f32)
        if j == 0:
            variant = jnp.where(is_first, 1, 0)
        elif j == n_blk - 1:
            variant = jnp.where(is_last, 2, 0)
        else:
            variant = 0
        return s + bias_ref[variant, hk]

    def softmax(s, hk):
        sink_col = sink_ref[hk]
        m = jnp.maximum(jnp.max(s, axis=-1, keepdims=True), sink_col)
        p = jnp.exp2(s - m)
        denom = jnp.sum(p, axis=-1, keepdims=True) + jnp.exp2(sink_col - m)
        return p.astype(bf16), denom

    def weighted_values(p, denom, hk, j):
        rows = slice(j * BLOCK, (j + 1) * BLOCK)
        vwin = v_scr[hk, j * BLOCK:j * BLOCK + 3 * BLOCK, :]
        o = jnp.dot(p, vwin, preferred_element_type=f32) / denom
        for pair in range(GROUP // 2):
            even = o[(2 * pair) * BLOCK:(2 * pair + 1) * BLOCK]
            odd = o[(2 * pair + 1) * BLOCK:(2 * pair + 2) * BLOCK]
            col = (hk * GROUP // 2 + pair) * 2 * HEAD_DIM
            attn_scr[rows, col:col + 2 * HEAD_DIM] = jnp.where(low, even, odd)

    def pooled_group(u, gi):
        w = POOL_WINDOWS[gi]
        r0, n = POOL_LEAD, TILE
        if w == 2:
            wsum = u[r0 - 1:r0 - 1 + n] + u[r0:r0 + n]
        else:
            c = u[0:n + 32] + u[1:n + 33]
            if w == 4:
                wsum = c[r0 - 2:r0 - 2 + n] + c[r0:r0 + n]
            else:
                e = c[0:n + 24] + c[2:n + 26]
                if w == 8:
                    wsum = e[r0 - 4:r0 - 4 + n] + e[r0:r0 + n]
                else:
                    f = e[0:n + 16] + e[4:n + 20]
                    wsum = f[r0 - 8:r0 - 8 + n] + f[r0:r0 + n]
        edge = POOL_EDGE_ROWS
        head = wsum[:edge] * cnt_ref[0, jnp.where(is_first, 1, 0), gi]
        body = wsum[edge:n - edge] * (1.0 / w)
        tail = wsum[n - edge:] * cnt_ref[1, jnp.where(is_last, 1, 0), gi]
        pooled = jnp.concatenate([head, body, tail], axis=0) - u[r0:r0 + n]
        pool_w = pool_w_ref[gi * POOL_GROUP:(gi + 1) * POOL_GROUP, :]
        return jnp.dot(pooled.astype(bf16), pool_w, preferred_element_type=f32)

    centre = slice(HALO, HALO + TILE)
    pool_rows = slice(HALO - POOL_LEAD, HALO + TILE + POOL_TAIL)
    work = [(hk, j) for j in range(n_blk) for hk in range(N_KV_HEADS)]
    fillers = (proj_chunks(centre, OFF_GA, D_MODEL, _sigmoid)
               + proj_chunks(centre, OFF_GB, D_MODEL, _sigmoid)
               + proj_chunks(centre, OFF_ZA, ATTN_WIDTH)
               + proj_chunks(pool_rows, OFF_U, POOL_WIDTH)
               + proj_chunks(centre, OFF_ZB, POOL_WIDTH))
    filled = []
    s_list = [scores(*work[n]) for n in range(SCORE_LEAD)]
    p_prev = None
    for n, (hk, j) in enumerate(work):
        filled += [f() for f in fillers[FILL_PER_STEP * n:FILL_PER_STEP * (n + 1)]]
        if n + SCORE_LEAD < len(work):
            s_list.append(scores(*work[n + SCORE_LEAD]))
        p_cur = softmax(s_list[n], hk)
        if p_prev is not None:
            weighted_values(*p_prev, *work[n - 1])
        p_prev = p_cur
    filled += [f() for f in fillers[FILL_PER_STEP * len(work):]]
    weighted_values(*p_prev, *work[-1])

    def take(width):
        chunks = [filled.pop(0) for _ in range(width // MXU_COLS)]
        return jnp.concatenate(chunks, axis=1)

    gate_a = take(D_MODEL)
    gate_b = take(D_MODEL)
    z_a = take(ATTN_WIDTH)
    u_all = take(POOL_WIDTH)
    z_b = take(POOL_WIDTH)
    u_all = jnp.concatenate([jnp.where(is_first, 0.0, u_all[:POOL_LEAD]),
                             u_all[POOL_LEAD:POOL_LEAD + TILE],
                             jnp.where(is_last, 0.0, u_all[POOL_LEAD + TILE:])], axis=0)

    gated_a = (attn_scr[...] * _silu(z_a)).astype(bf16)
    y_a = jnp.dot(gated_a, w_a_ref[...], preferred_element_type=f32)
    mixed = [pooled_group(u_all[:, gi * POOL_GROUP:(gi + 1) * POOL_GROUP], gi)
             for gi in range(len(POOL_WINDOWS))]
    mixed = jnp.concatenate(mixed, axis=1) * pool_scale_ref[...]
    gated_b = (mixed * _silu(z_b)).astype(bf16)
    y_b = jnp.dot(gated_b, w_b_ref[...], preferred_element_type=f32)
    t_a = gate_a * y_a

    final_g = final_g_ref[...]
    for r in range(0, TILE, TAIL_ROWS):
        rows = slice(r, r + TAIL_ROWS)
        merged = (t_a[rows] + gate_b[rows] * y_b[rows]).astype(bf16)
        y = x_cur_ref[rows, :] + jnp.dot(merged, w_out_ref[...], preferred_element_type=f32)
        out_ref[rows, :] = _rms_norm(y, final_g)

    h_new = jnp.concatenate([normed(x_nxt_ref, r) for r in range(HALO, TILE, BLOCK)]
                            + [normed(x_after_ref, 0)], axis=0)
    h_keep = h_scr[TILE:ROWS, :]
    k_keep, v_keep = k_scr[:, TILE:ROWS, :], v_scr[:, TILE:ROWS, :]
    h_scr[0:2 * HALO, :] = h_keep
    h_scr[2 * HALO:ROWS, :] = h_new
    k_scr[:, 0:2 * HALO, :] = k_keep
    v_scr[:, 0:2 * HALO, :] = v_keep
    for r in range(0, TILE, 2 * HALO):
        store_kv(h_new[r:r + 2 * HALO], 2 * HALO + r)
    store_q(jnp.concatenate([h_keep[HALO:], h_new[:TILE - HALO]], axis=0))


def kernel(x, norm_g, w_in, attn_sink, pool_w, pool_scale, w_branch_a, w_branch_b, w_out, final_norm_g):
    batch, seq_len, d_model = x.shape
    depth = norm_g.shape[0]
    assert depth == 1, "the final RMSNorm is fused into the single layer's kernel"
    assert d_model == D_MODEL and w_in.shape[-1] == IN_WIDTH
    assert seq_len % TILE == 0 and seq_len // BLOCK >= 2 and seq_len >= 2 * POOL_EDGE_ROWS
    assert POOL_LEAD >= max(POOL_WINDOWS) // 2 and POOL_EDGE_ROWS >= max(POOL_WINDOWS) // 2
    assert TILE // BLOCK >= 2 and TILE // TAIL_ROWS >= 2
    bf16 = jnp.bfloat16
    bias = jnp.asarray(_attention_bias())
    inv_count = jnp.asarray(_pool_inv_count(seq_len))
    tiles_per_seq = seq_len // TILE
    n_steps = batch * tiles_per_seq
    blk_per_tile = TILE // BLOCK
    last_blk = n_steps * blk_per_tile - 1
    pool_rows = len(POOL_WINDOWS) * POOL_GROUP

    def resident(shape):
        return pl.BlockSpec(shape, lambda s: (0,) * len(shape), pipeline_mode=pl.Buffered(1))

    in_hbm = pl.BlockSpec(memory_space=pl.ANY)
    call = pl.pallas_call(
        functools.partial(_block_kernel, tiles_per_seq=tiles_per_seq),
        grid=(n_steps,),
        in_specs=[
            pl.BlockSpec((TILE, D_MODEL), lambda s: (s, 0)),
            pl.BlockSpec((TILE, D_MODEL), lambda s: (jnp.minimum(s + 1, n_steps - 1), 0)),
            pl.BlockSpec((BLOCK, D_MODEL), lambda s: (jnp.minimum((s + 2) * blk_per_tile, last_blk), 0)),
            resident((1, D_MODEL)),
            in_hbm,
            resident((N_KV_HEADS, GROUP * BLOCK, 1)),
            resident((3, N_KV_HEADS, GROUP * BLOCK, 3 * BLOCK)),
            resident(inv_count.shape),
            in_hbm,
            resident((1, POOL_WIDTH)),
            in_hbm,
            in_hbm,
            in_hbm,
            resident((1, D_MODEL)),
        ],
        out_specs=pl.BlockSpec((TILE, D_MODEL), lambda s: (s, 0)),
        out_shape=jax.ShapeDtypeStruct((batch * seq_len, d_model), x.dtype),
        scratch_shapes=[
            pltpu.VMEM((D_MODEL, IN_WIDTH), bf16),
            pltpu.VMEM((pool_rows, POOL_GROUP), bf16),
            pltpu.VMEM((ATTN_WIDTH, D_MODEL), bf16),
            pltpu.VMEM((POOL_WIDTH, D_MODEL), bf16),
            pltpu.VMEM((D_MODEL, D_MODEL), bf16),
            pltpu.VMEM((W_SLOTS, W_IN_CHUNK, IN_WIDTH), jnp.float32),
            pltpu.VMEM((W_SLOTS, W_SQ_CHUNK, D_MODEL), jnp.float32),
            pltpu.VMEM((W_SLOTS, W_SQ_CHUNK, POOL_GROUP), jnp.float32),
            pltpu.SemaphoreType.DMA((3, W_SLOTS)),
            pltpu.VMEM((ROWS, D_MODEL), bf16),
            pltpu.VMEM((N_KV_HEADS, ROWS, 2 * HEAD_DIM), bf16),
            pltpu.VMEM((N_KV_HEADS, ROWS, 2 * HEAD_DIM), bf16),
            pltpu.VMEM((TILE, ATTN_WIDTH), bf16),
            pltpu.VMEM((TILE, ATTN_WIDTH), jnp.float32),
        ],
        compiler_params=pltpu.CompilerParams(
            dimension_semantics=("arbitrary",),
            vmem_limit_bytes=VMEM_LIMIT_BYTES),
        name="hybrid_block",
    )

    x2d = x.reshape(batch * seq_len, d_model)
    sink_col = jnp.repeat(attn_sink[0].astype(jnp.float32) * LOG2_E, BLOCK).reshape(N_KV_HEADS, GROUP * BLOCK, 1)
    y2d = call(x2d, x2d, x2d,
               norm_g[0].reshape(1, D_MODEL), w_in[0], sink_col, bias, inv_count,
               pool_w[0].reshape(pool_rows, POOL_GROUP), pool_scale[0].reshape(1, POOL_WIDTH),
               w_branch_a[0], w_branch_b[0], w_out[0],
               final_norm_g.reshape(1, D_MODEL))
    return y2d.reshape(batch, seq_len, d_model)
```

```python
import functools

import numpy as np
import jax
import jax.numpy as jnp
from jax import lax
from jax.experimental import pallas as pl
from jax.experimental.pallas import tpu as pltpu

D_MODEL = 1024
N_Q_HEADS = 8
N_KV_HEADS = 2
GROUP = N_Q_HEADS // N_KV_HEADS
HEAD_DIM = 64
ATTN_WIDTH = N_Q_HEADS * HEAD_DIM
KV_WIDTH = N_KV_HEADS * HEAD_DIM
WINDOW = 128
BLOCK = 128
POOL_WIDTH = 512
POOL_WINDOWS = (2, 4, 8, 16)
POOL_GROUP = 128
EPS = 1e-6
NEG_INF = -1e30
LOG2_E = 1.4426950408889634

OFF_Q = 0
OFF_K = OFF_Q + ATTN_WIDTH
OFF_V = OFF_K + KV_WIDTH
OFF_ZA = OFF_V + KV_WIDTH
OFF_U = OFF_ZA + ATTN_WIDTH
OFF_ZB = OFF_U + POOL_WIDTH
OFF_GA = OFF_ZB + POOL_WIDTH
OFF_GB = OFF_GA + D_MODEL
IN_WIDTH = OFF_GB + D_MODEL

TILE = 512
HALO = BLOCK
ROWS = TILE + 2 * HALO
MXU_COLS = 256
TAIL_ROWS = 256
SCORE_LEAD = 2
FILL_PER_STEP = 1
POOL_EDGE_ROWS = 16
POOL_LEAD = 16
POOL_TAIL = 32
W_IN_CHUNK = 64
W_SQ_CHUNK = 128
W_SLOTS = 3
VMEM_LIMIT_BYTES = 56 * 1024 * 1024


def _attention_bias():
    r = np.arange(BLOCK)[:, None]
    c = np.arange(3 * BLOCK)[None, :]
    dist = np.abs(r - (c - BLOCK))
    in_window = dist <= WINDOW
    slopes = np.exp2(-8.0 * np.arange(1, N_Q_HEADS + 1, dtype=np.float64) / N_Q_HEADS)
    alibi = -slopes[:, None, None] * dist[None].astype(np.float64)
    valid = [np.ones_like(c, bool), c >= BLOCK, c < 2 * BLOCK]
    out = np.stack([np.where(in_window & v, alibi * LOG2_E, NEG_INF) for v in valid])
    return out.reshape(3, N_KV_HEADS, GROUP * BLOCK, 3 * BLOCK).astype(np.float32)


def _pool_inv_count(seq_len):
    out = np.zeros((2, 2, len(POOL_WINDOWS), POOL_EDGE_ROWS, 1), np.float64)
    for gi, w in enumerate(POOL_WINDOWS):
        half = w // 2
        out[:, 0, gi] = 1.0 / w
        for end, pos in enumerate([np.arange(POOL_EDGE_ROWS), seq_len - POOL_EDGE_ROWS + np.arange(POOL_EDGE_ROWS)]):
            count = np.minimum(pos + half, seq_len) - np.maximum(pos - half, 0)
            out[end, 1, gi, :, 0] = 1.0 / count
    return np.broadcast_to(out, out.shape[:-1] + (POOL_GROUP,)).astype(np.float32)


def _rms_norm(x, g):
    ms = jnp.mean(x * x, axis=-1, keepdims=True)
    return x * lax.rsqrt(ms + EPS) * g


def _sigmoid(x):
    return 0.5 * jnp.tanh(0.5 * x) + 0.5


def _silu(x):
    return x * _sigmoid(x)


def _cast_weights(streams):
    rings = []
    for pairs, stage_ref, sem_ref in streams:
        chunk = stage_ref.shape[1]
        jobs = [(src, dst, r) for src, dst in pairs for r in range(0, dst.shape[0], chunk)]
        rings.append((jobs, stage_ref, sem_ref))

    def copy(ring, c):
        jobs, stage_ref, sem_ref = ring
        src, _, r = jobs[c]
        slot = c % stage_ref.shape[0]
        return pltpu.make_async_copy(src.at[pl.ds(r, stage_ref.shape[1]), :], stage_ref.at[slot], sem_ref.at[slot])

    for ring in rings:
        for c in range(min(ring[1].shape[0], len(ring[0]))):
            copy(ring, c).start()
    for c in range(max(len(ring[0]) for ring in rings)):
        for ring in rings:
            jobs, stage_ref, _ = ring
            if c < len(jobs):
                n_slots, chunk = stage_ref.shape[0], stage_ref.shape[1]
                _, dst, r = jobs[c]
                copy(ring, c).wait()
                dst[r:r + chunk, :] = stage_ref[c % n_slots].astype(dst.dtype)
                if c + n_slots < len(jobs):
                    copy(ring, c + n_slots).start()


def _block_kernel(x_first_ref, x_nxt_ref, x_after_ref, norm_g_ref, w_in_hbm, sink_ref, bias_ref, cnt_ref,
                  pool_w_hbm, pool_scale_ref, w_a_hbm, w_b_hbm, w_out_hbm, final_g_ref,
                  out_ref, w_in_ref, pool_w_ref, w_a_ref, w_b_ref, w_out_ref, stage_in, stage_sq, stage_pool, sems,
                  x_scr, h_scr, k_scr, v_scr, q_scr, attn_scr, *, tiles_per_seq):
    f32, bf16 = jnp.float32, jnp.bfloat16
    step = pl.program_id(0)
    i = lax.rem(step, tiles_per_seq)
    n_blk = TILE // BLOCK
    is_first = i == 0
    is_last = i == tiles_per_seq - 1
    norm_g = norm_g_ref[...]
    lane = lax.broadcasted_iota(jnp.int32, (1, 2 * HEAD_DIM), 1)
    low = lane < HEAD_DIM

    def proj(h, off, width):
        return jnp.dot(h, w_in_ref[:, off:off + width], preferred_element_type=f32)

    def proj_chunks(rows, off, width, act=lambda a: a):
        return [lambda c=c: act(proj(h_scr[rows, :], off + c, MXU_COLS)) for c in range(0, width, MXU_COLS)]

    def normed(x_ref, r):
        return _rms_norm(x_ref[r:r + BLOCK, :], norm_g).astype(bf16)

    def store_kv(h_rows, dst):
        n = h_rows.shape[0]
        kv = proj(h_rows, OFF_K, 2 * KV_WIDTH)
        for scr, a in ((k_scr, kv[:, :KV_WIDTH]), (v_scr, kv[:, KV_WIDTH:])):
            swapped = pltpu.roll(a, HEAD_DIM, axis=1)
            scr[0, dst:dst + n, :] = jnp.where(low, a, swapped).astype(bf16)
            scr[1, dst:dst + n, :] = jnp.where(low, swapped, a).astype(bf16)

    def store_q(h_rows):
        q_scr[...] = (proj(h_rows, OFF_Q, ATTN_WIDTH) * (HEAD_DIM ** -0.5 * LOG2_E)).astype(bf16)

    @pl.when(step == 0)
    def _prepare_first_tile():
        _cast_weights([
            ([(w_in_hbm, w_in_ref)], stage_in, sems.at[0]),
            ([(w_out_hbm, w_out_ref), (w_a_hbm, w_a_ref), (w_b_hbm, w_b_ref)], stage_sq, sems.at[1]),
            ([(pool_w_hbm, pool_w_ref)], stage_pool, sems.at[2]),
        ])
        zeros = jnp.zeros((HALO, D_MODEL), bf16)
        x_scr[...] = x_first_ref[...]
        h_new = jnp.concatenate([normed(x_first_ref, r) for r in range(0, TILE, BLOCK)]
                                + [normed(x_nxt_ref, 0)], axis=0)
        h_scr[0:HALO, :] = zeros
        h_scr[HALO:ROWS, :] = h_new
        store_kv(zeros, 0)
        store_kv(h_new, HALO)
        store_q(h_new[0:TILE])

    zero = jnp.zeros((), bf16)

    def scores(hk, j):
        rows = slice(j * BLOCK, (j + 1) * BLOCK)
        pieces = []
        for pair in range(GROUP // 2):
            col = (hk * GROUP // 2 + pair) * 2 * HEAD_DIM
            qp = q_scr[rows, col:col + 2 * HEAD_DIM]
            pieces.append(jnp.where(low, qp, zero))
            pieces.append(jnp.where(low, zero, qp))
        lhs = jnp.concatenate(pieces, axis=0)
        kwin = k_scr[hk, j * BLOCK:j * BLOCK + 3 * BLOCK, :]
        s = lax.dot_general(lhs, kwin, (((1,), (1,)), ((), ())), preferred_element_type=f32)
        if j == 0:
            variant = jnp.where(is_first, 1, 0)
        elif j == n_blk - 1:
            variant = jnp.where(is_last, 2, 0)
        else:
            variant = 0
        return s + bias_ref[variant, hk]

    def softmax(s, hk):
        sink_col = sink_ref[hk]
        m = jnp.maximum(jnp.max(s, axis=-1, keepdims=True), sink_col)
        p = jnp.exp2(s - m)
        denom = jnp.sum(p, axis=-1, keepdims=True) + jnp.exp2(sink_col - m)
        return p.astype(bf16), denom

    def weighted_values(p, denom, hk, j):
        rows = slice(j * BLOCK, (j + 1) * BLOCK)
        vwin = v_scr[hk, j * BLOCK:j * BLOCK + 3 * BLOCK, :]
        o = jnp.dot(p, vwin, preferred_element_type=f32) / denom
        for pair in range(GROUP // 2):
            even = o[(2 * pair) * BLOCK:(2 * pair + 1) * BLOCK]
            odd = o[(2 * pair + 1) * BLOCK:(2 * pair + 2) * BLOCK]
            col = (hk * GROUP // 2 + pair) * 2 * HEAD_DIM
            attn_scr[rows, col:col + 2 * HEAD_DIM] = jnp.where(low, even, odd)

    def pooled_group(u, gi):
        w = POOL_WINDOWS[gi]
        r0, n = POOL_LEAD, TILE
        if w == 2:
            wsum = u[r0 - 1:r0 - 1 + n] + u[r0:r0 + n]
        else:
            c = u[0:n + 32] + u[1:n + 33]
            if w == 4:
                wsum = c[r0 - 2:r0 - 2 + n] + c[r0:r0 + n]
            else:
                e = c[0:n + 24] + c[2:n + 26]
                if w == 8:
                    wsum = e[r0 - 4:r0 - 4 + n] + e[r0:r0 + n]
                else:
                    f = e[0:n + 16] + e[4:n + 20]
                    wsum = f[r0 - 8:r0 - 8 + n] + f[r0:r0 + n]
        edge = POOL_EDGE_ROWS
        head = wsum[:edge] * cnt_ref[0, jnp.where(is_first, 1, 0), gi]
        body = wsum[edge:n - edge] * (1.0 / w)
        tail = wsum[n - edge:] * cnt_ref[1, jnp.where(is_last, 1, 0), gi]
        pooled = jnp.concatenate([head, body, tail], axis=0) - u[r0:r0 + n]
        pool_w = pool_w_ref[gi * POOL_GROUP:(gi + 1) * POOL_GROUP, :]
        return jnp.dot(pooled.astype(bf16), pool_w, preferred_element_type=f32)

    centre = slice(HALO, HALO + TILE)
    pool_rows = slice(HALO - POOL_LEAD, HALO + TILE + POOL_TAIL)
    work = [(hk, j) for j in range(n_blk) for hk in range(N_KV_HEADS)]
    fillers = (proj_chunks(centre, OFF_GA, D_MODEL, _sigmoid)
               + proj_chunks(centre, OFF_GB, D_MODEL, _sigmoid)
               + proj_chunks(centre, OFF_ZA, ATTN_WIDTH)
               + proj_chunks(pool_rows, OFF_U, POOL_WIDTH)
               + proj_chunks(centre, OFF_ZB, POOL_WIDTH))
    filled = []
    s_list = [scores(*work[n]) for n in range(SCORE_LEAD)]
    p_prev = None
    for n, (hk, j) in enumerate(work):
        filled += [f() for f in fillers[FILL_PER_STEP * n:FILL_PER_STEP * (n + 1)]]
        if n + SCORE_LEAD < len(work):
            s_list.append(scores(*work[n + SCORE_LEAD]))
        p_cur = softmax(s_list[n], hk)
        if p_prev is not None:
            weighted_values(*p_prev, *work[n - 1])
        p_prev = p_cur
    filled += [f() for f in fillers[FILL_PER_STEP * len(work):]]
    weighted_values(*p_prev, *work[-1])

    def take(width):
        chunks = [filled.pop(0) for _ in range(width // MXU_COLS)]
        return jnp.concatenate(chunks, axis=1)

    gate_a = take(D_MODEL)
    gate_b = take(D_MODEL)
    z_a = take(ATTN_WIDTH)
    u_all = take(POOL_WIDTH)
    z_b = take(POOL_WIDTH)
    u_all = jnp.concatenate([jnp.where(is_first, 0.0, u_all[:POOL_LEAD]),
                             u_all[POOL_LEAD:POOL_LEAD + TILE],
                             jnp.where(is_last, 0.0, u_all[POOL_LEAD + TILE:])], axis=0)

    gated_a = (attn_scr[...] * _silu(z_a)).astype(bf16)
    y_a = jnp.dot(gated_a, w_a_ref[...], preferred_element_type=f32)
    mixed = [pooled_group(u_all[:, gi * POOL_GROUP:(gi + 1) * POOL_GROUP], gi)
             for gi in range(len(POOL_WINDOWS))]
    mixed = jnp.concatenate(mixed, axis=1) * pool_scale_ref[...]
    gated_b = (mixed * _silu(z_b)).astype(bf16)
    y_b = jnp.dot(gated_b, w_b_ref[...], preferred_element_type=f32)
    t_a = gate_a * y_a

    final_g = final_g_ref[...]
    for r in range(0, TILE, TAIL_ROWS):
        rows = slice(r, r + TAIL_ROWS)
        merged = (t_a[rows] + gate_b[rows] * y_b[rows]).astype(bf16)
        y = x_scr[rows, :] + jnp.dot(merged, w_out_ref[...], preferred_element_type=f32)
        out_ref[rows, :] = _rms_norm(y, final_g)
    x_scr[...] = x_nxt_ref[...]

    h_new = jnp.concatenate([normed(x_nxt_ref, r) for r in range(HALO, TILE, BLOCK)]
                            + [normed(x_after_ref, 0)], axis=0)
    h_keep = h_scr[TILE:ROWS, :]
    k_keep, v_keep = k_scr[:, TILE:ROWS, :], v_scr[:, TILE:ROWS, :]
    h_scr[0:2 * HALO, :] = h_keep
    h_scr[2 * HALO:ROWS, :] = h_new
    k_scr[:, 0:2 * HALO, :] = k_keep
    v_scr[:, 0:2 * HALO, :] = v_keep
    for r in range(0, TILE, 2 * HALO):
        store_kv(h_new[r:r + 2 * HALO], 2 * HALO + r)
    store_q(jnp.concatenate([h_keep[HALO:], h_new[:TILE - HALO]], axis=0))


def kernel(x, norm_g, w_in, attn_sink, pool_w, pool_scale, w_branch_a, w_branch_b, w_out, final_norm_g):
    batch, seq_len, d_model = x.shape
    depth = norm_g.shape[0]
    assert depth == 1, "the final RMSNorm is fused into the single layer's kernel"
    assert d_model == D_MODEL and w_in.shape[-1] == IN_WIDTH
    assert seq_len % TILE == 0 and seq_len // BLOCK >= 2 and seq_len >= 2 * POOL_EDGE_ROWS
    assert POOL_LEAD >= max(POOL_WINDOWS) // 2 and POOL_EDGE_ROWS >= max(POOL_WINDOWS) // 2
    assert TILE // BLOCK >= 2 and TILE // TAIL_ROWS >= 2
    bf16 = jnp.bfloat16
    bias = jnp.asarray(_attention_bias())
    inv_count = jnp.asarray(_pool_inv_count(seq_len))
    tiles_per_seq = seq_len // TILE
    n_steps = batch * tiles_per_seq
    blk_per_tile = TILE // BLOCK
    last_blk = n_steps * blk_per_tile - 1
    pool_rows = len(POOL_WINDOWS) * POOL_GROUP

    def resident(shape):
        return pl.BlockSpec(shape, lambda s: (0,) * len(shape), pipeline_mode=pl.Buffered(1))

    in_hbm = pl.BlockSpec(memory_space=pl.ANY)
    call = pl.pallas_call(
        functools.partial(_block_kernel, tiles_per_seq=tiles_per_seq),
        grid=(n_steps,),
        in_specs=[
            resident((TILE, D_MODEL)),
            pl.BlockSpec((TILE, D_MODEL), lambda s: (jnp.minimum(s + 1, n_steps - 1), 0)),
            pl.BlockSpec((BLOCK, D_MODEL), lambda s: (jnp.minimum((s + 2) * blk_per_tile, last_blk), 0)),
            resident((1, D_MODEL)),
            in_hbm,
            resident((N_KV_HEADS, GROUP * BLOCK, 1)),
            resident((3, N_KV_HEADS, GROUP * BLOCK, 3 * BLOCK)),
            resident(inv_count.shape),
            in_hbm,
            resident((1, POOL_WIDTH)),
            in_hbm,
            in_hbm,
            in_hbm,
            resident((1, D_MODEL)),
        ],
        out_specs=pl.BlockSpec((TILE, D_MODEL), lambda s: (s, 0)),
        out_shape=jax.ShapeDtypeStruct((batch * seq_len, d_model), x.dtype),
        scratch_shapes=[
            pltpu.VMEM((D_MODEL, IN_WIDTH), bf16),
            pltpu.VMEM((pool_rows, POOL_GROUP), bf16),
            pltpu.VMEM((ATTN_WIDTH, D_MODEL), bf16),
            pltpu.VMEM((POOL_WIDTH, D_MODEL), bf16),
            pltpu.VMEM((D_MODEL, D_MODEL), bf16),
            pltpu.VMEM((W_SLOTS, W_IN_CHUNK, IN_WIDTH), jnp.float32),
            pltpu.VMEM((W_SLOTS, W_SQ_CHUNK, D_MODEL), jnp.float32),
            pltpu.VMEM((W_SLOTS, W_SQ_CHUNK, POOL_GROUP), jnp.float32),
            pltpu.SemaphoreType.DMA((3, W_SLOTS)),
            pltpu.VMEM((TILE, D_MODEL), jnp.float32),
            pltpu.VMEM((ROWS, D_MODEL), bf16),
            pltpu.VMEM((N_KV_HEADS, ROWS, 2 * HEAD_DIM), bf16),
            pltpu.VMEM((N_KV_HEADS, ROWS, 2 * HEAD_DIM), bf16),
            pltpu.VMEM((TILE, ATTN_WIDTH), bf16),
            pltpu.VMEM((TILE, ATTN_WIDTH), jnp.float32),
        ],
        compiler_params=pltpu.CompilerParams(
            dimension_semantics=("arbitrary",),
            vmem_limit_bytes=VMEM_LIMIT_BYTES),
        name="hybrid_block",
    )

    x2d = x.reshape(batch * seq_len, d_model)
    sink_col = jnp.repeat(attn_sink[0].astype(jnp.float32) * LOG2_E, BLOCK).reshape(N_KV_HEADS, GROUP * BLOCK, 1)
    y2d = call(x2d, x2d, x2d,
               norm_g[0].reshape(1, D_MODEL), w_in[0], sink_col, bias, inv_count,
               pool_w[0].reshape(pool_rows, POOL_GROUP), pool_scale[0].reshape(1, POOL_WIDTH),
               w_branch_a[0], w_branch_b[0], w_out[0],
               final_norm_g.reshape(1, D_MODEL))
    return y2d.reshape(batch, seq_len, d_model)
```

```python
import functools

import numpy as np
import jax
import jax.numpy as jnp
from jax import lax
from jax.experimental import pallas as pl
from jax.experimental.pallas import tpu as pltpu

D_MODEL = 1024
N_Q_HEADS = 8
N_KV_HEADS = 2
GROUP = N_Q_HEADS // N_KV_HEADS
HEAD_DIM = 64
ATTN_WIDTH = N_Q_HEADS * HEAD_DIM
KV_WIDTH = N_KV_HEADS * HEAD_DIM
WINDOW = 128
BLOCK = 128
POOL_WIDTH = 512
POOL_WINDOWS = (2, 4, 8, 16)
POOL_GROUP = 128
EPS = 1e-6
NEG_INF = -1e30
LOG2_E = 1.4426950408889634

OFF_Q = 0
OFF_K = OFF_Q + ATTN_WIDTH
OFF_V = OFF_K + KV_WIDTH
OFF_ZA = OFF_V + KV_WIDTH
OFF_U = OFF_ZA + ATTN_WIDTH
OFF_ZB = OFF_U + POOL_WIDTH
OFF_GA = OFF_ZB + POOL_WIDTH
OFF_GB = OFF_GA + D_MODEL
IN_WIDTH = OFF_GB + D_MODEL

TILE = 512
HALO = BLOCK
ROWS = TILE + 2 * HALO
MXU_COLS = 256
TAIL_ROWS = 256
SCORE_LEAD = 2
FILL_PER_STEP = 1
POOL_EDGE_ROWS = 16
POOL_LEAD = 16
POOL_TAIL = 32
W_IN_CHUNK = 64
W_SQ_CHUNK = 128
W_SLOTS = 3
VMEM_LIMIT_BYTES = 56 * 1024 * 1024


def _attention_bias():
    r = np.arange(BLOCK)[:, None]
    c = np.arange(3 * BLOCK)[None, :]
    dist = np.abs(r - (c - BLOCK))
    in_window = dist <= WINDOW
    slopes = np.exp2(-8.0 * np.arange(1, N_Q_HEADS + 1, dtype=np.float64) / N_Q_HEADS)
    alibi = -slopes[:, None, None] * dist[None].astype(np.float64)
    valid = [np.ones_like(c, bool), c >= BLOCK, c < 2 * BLOCK]
    out = np.stack([np.where(in_window & v, alibi * LOG2_E, NEG_INF) for v in valid])
    return out.reshape(3, N_KV_HEADS, GROUP * BLOCK, 3 * BLOCK).astype(np.float32)


def _pool_inv_count(seq_len):
    out = np.zeros((2, 2, len(POOL_WINDOWS), POOL_EDGE_ROWS, 1), np.float64)
    for gi, w in enumerate(POOL_WINDOWS):
        half = w // 2
        out[:, 0, gi] = 1.0 / w
        for end, pos in enumerate([np.arange(POOL_EDGE_ROWS), seq_len - POOL_EDGE_ROWS + np.arange(POOL_EDGE_ROWS)]):
            count = np.minimum(pos + half, seq_len) - np.maximum(pos - half, 0)
            out[end, 1, gi, :, 0] = 1.0 / count
    return np.broadcast_to(out, out.shape[:-1] + (POOL_GROUP,)).astype(np.float32)


def _rms_norm(x, g):
    ms = jnp.mean(x * x, axis=-1, keepdims=True)
    return x * lax.rsqrt(ms + EPS) * g


def _sigmoid(x):
    return 0.5 * jnp.tanh(0.5 * x) + 0.5


def _silu(x):
    return x * _sigmoid(x)


def _cast_weights(streams):
    rings = []
    for pairs, stage_ref, sem_ref in streams:
        chunk = stage_ref.shape[1]
        jobs = [(src, dst, r) for src, dst in pairs for r in range(0, dst.shape[0], chunk)]
        rings.append((jobs, stage_ref, sem_ref))

    def copy(ring, c):
        jobs, stage_ref, sem_ref = ring
        src, _, r = jobs[c]
        slot = c % stage_ref.shape[0]
        return pltpu.make_async_copy(src.at[pl.ds(r, stage_ref.shape[1]), :], stage_ref.at[slot], sem_ref.at[slot])

    for ring in rings:
        for c in range(min(ring[1].shape[0], len(ring[0]))):
            copy(ring, c).start()
    for c in range(max(len(ring[0]) for ring in rings)):
        for ring in rings:
            jobs, stage_ref, _ = ring
            if c < len(jobs):
                n_slots, chunk = stage_ref.shape[0], stage_ref.shape[1]
                _, dst, r = jobs[c]
                copy(ring, c).wait()
                dst[r:r + chunk, :] = stage_ref[c % n_slots].astype(dst.dtype)
                if c + n_slots < len(jobs):
                    copy(ring, c + n_slots).start()


def _block_kernel(x_cur_ref, x_nxt_ref, x_after_ref, norm_g_ref, w_in_hbm, sink_ref, bias_ref, cnt_ref,
                  pool_w_hbm, pool_scale_ref, w_a_hbm, w_b_hbm, w_out_hbm, final_g_ref,
                  out_ref, w_in_ref, pool_w_ref, w_a_ref, w_b_ref, w_out_ref, stage_in, stage_sq, stage_pool, sems,
                  h_scr, k_scr, v_scr, q_scr, attn_scr, *, tiles_per_seq):
    f32, bf16 = jnp.float32, jnp.bfloat16
    step = pl.program_id(0)
    i = lax.rem(step, tiles_per_seq)
    n_blk = TILE // BLOCK
    is_first = i == 0
    is_last = i == tiles_per_seq - 1
    norm_g = norm_g_ref[...]
    lane = lax.broadcasted_iota(jnp.int32, (1, 2 * HEAD_DIM), 1)
    low = lane < HEAD_DIM

    def proj(h, off, width):
        return jnp.dot(h, w_in_ref[:, off:off + width], preferred_element_type=f32)

    def proj_chunks(rows, off, width, act=lambda a: a):
        return [lambda c=c: act(proj(h_scr[rows, :], off + c, MXU_COLS)) for c in range(0, width, MXU_COLS)]

    def normed(x_ref, r):
        return _rms_norm(x_ref[r:r + BLOCK, :], norm_g).astype(bf16)

    def store_kv(h_rows, dst):
        n = h_rows.shape[0]
        kv = proj(h_rows, OFF_K, 2 * KV_WIDTH)
        k, v = kv[:, :KV_WIDTH], kv[:, KV_WIDTH:]
        k_sw, v_sw = pltpu.roll(k, HEAD_DIM, axis=1), pltpu.roll(v, HEAD_DIM, axis=1)
        v_scr[0, dst:dst + n, :] = jnp.where(low, v, v_sw).astype(bf16)
        v_scr[1, dst:dst + n, :] = jnp.where(low, v_sw, v).astype(bf16)
        k_scr[0, :, dst:dst + n] = jnp.where(low, k, k_sw).T.astype(bf16)
        k_scr[1, :, dst:dst + n] = jnp.where(low, k_sw, k).T.astype(bf16)

    def store_q(h_rows):
        q_scr[...] = (proj(h_rows, OFF_Q, ATTN_WIDTH) * (HEAD_DIM ** -0.5 * LOG2_E)).astype(bf16)

    @pl.when(step == 0)
    def _prepare_first_tile():
        _cast_weights([
            ([(w_in_hbm, w_in_ref)], stage_in, sems.at[0]),
            ([(w_out_hbm, w_out_ref), (w_a_hbm, w_a_ref), (w_b_hbm, w_b_ref)], stage_sq, sems.at[1]),
            ([(pool_w_hbm, pool_w_ref)], stage_pool, sems.at[2]),
        ])
        zeros = jnp.zeros((HALO, D_MODEL), bf16)
        h_new = jnp.concatenate([normed(x_cur_ref, r) for r in range(0, TILE, BLOCK)]
                                + [normed(x_nxt_ref, 0)], axis=0)
        h_scr[0:HALO, :] = zeros
        h_scr[HALO:ROWS, :] = h_new
        store_kv(zeros, 0)
        store_kv(h_new, HALO)
        store_q(h_new[0:TILE])

    zero = jnp.zeros((), bf16)

    def scores(hk, j):
        rows = slice(j * BLOCK, (j + 1) * BLOCK)
        pieces = []
        for pair in range(GROUP // 2):
            col = (hk * GROUP // 2 + pair) * 2 * HEAD_DIM
            qp = q_scr[rows, col:col + 2 * HEAD_DIM]
            pieces.append(jnp.where(low, qp, zero))
            pieces.append(jnp.where(low, zero, qp))
        lhs = jnp.concatenate(pieces, axis=0)
        kwin = k_scr[hk, :, j * BLOCK:j * BLOCK + 3 * BLOCK]
        s = jnp.dot(lhs, kwin, preferred_element_type=f32)
        if j == 0:
            variant = jnp.where(is_first, 1, 0)
        elif j == n_blk - 1:
            variant = jnp.where(is_last, 2, 0)
        else:
            variant = 0
        return s + bias_ref[variant, hk]

    def softmax(s, hk):
        sink_col = sink_ref[hk]
        m = jnp.maximum(jnp.max(s, axis=-1, keepdims=True), sink_col)
        p = jnp.exp2(s - m)
        denom = jnp.sum(p, axis=-1, keepdims=True) + jnp.exp2(sink_col - m)
        return p.astype(bf16), denom

    def weighted_values(p, denom, hk, j):
        rows = slice(j * BLOCK, (j + 1) * BLOCK)
        vwin = v_scr[hk, j * BLOCK:j * BLOCK + 3 * BLOCK, :]
        o = jnp.dot(p, vwin, preferred_element_type=f32) / denom
        for pair in range(GROUP // 2):
            even = o[(2 * pair) * BLOCK:(2 * pair + 1) * BLOCK]
            odd = o[(2 * pair + 1) * BLOCK:(2 * pair + 2) * BLOCK]
            col = (hk * GROUP // 2 + pair) * 2 * HEAD_DIM
            attn_scr[rows, col:col + 2 * HEAD_DIM] = jnp.where(low, even, odd)

    def pooled_group(u, gi):
        w = POOL_WINDOWS[gi]
        r0, n = POOL_LEAD, TILE
        if w == 2:
            wsum = u[r0 - 1:r0 - 1 + n] + u[r0:r0 + n]
        else:
            c = u[0:n + 32] + u[1:n + 33]
            if w == 4:
                wsum = c[r0 - 2:r0 - 2 + n] + c[r0:r0 + n]
            else:
                e = c[0:n + 24] + c[2:n + 26]
                if w == 8:
                    wsum = e[r0 - 4:r0 - 4 + n] + e[r0:r0 + n]
                else:
                    f = e[0:n + 16] + e[4:n + 20]
                    wsum = f[r0 - 8:r0 - 8 + n] + f[r0:r0 + n]
        edge = POOL_EDGE_ROWS
        head = wsum[:edge] * cnt_ref[0, jnp.where(is_first, 1, 0), gi]
        body = wsum[edge:n - edge] * (1.0 / w)
        tail = wsum[n - edge:] * cnt_ref[1, jnp.where(is_last, 1, 0), gi]
        pooled = jnp.concatenate([head, body, tail], axis=0) - u[r0:r0 + n]
        pool_w = pool_w_ref[gi * POOL_GROUP:(gi + 1) * POOL_GROUP, :]
        return jnp.dot(pooled.astype(bf16), pool_w, preferred_element_type=f32)

    centre = slice(HALO, HALO + TILE)
    pool_rows = slice(HALO - POOL_LEAD, HALO + TILE + POOL_TAIL)
    work = [(hk, j) for j in range(n_blk) for hk in range(N_KV_HEADS)]
    fillers = (proj_chunks(centre, OFF_GA, D_MODEL, _sigmoid)
               + proj_chunks(centre, OFF_GB, D_MODEL, _sigmoid)
               + proj_chunks(centre, OFF_ZA, ATTN_WIDTH)
               + proj_chunks(pool_rows, OFF_U, POOL_WIDTH)
               + proj_chunks(centre, OFF_ZB, POOL_WIDTH))
    filled = []
    s_list = [scores(*work[n]) for n in range(SCORE_LEAD)]
    p_prev = None
    for n, (hk, j) in enumerate(work):
        filled += [f() for f in fillers[FILL_PER_STEP * n:FILL_PER_STEP * (n + 1)]]
        if n + SCORE_LEAD < len(work):
            s_list.append(scores(*work[n + SCORE_LEAD]))
        p_cur = softmax(s_list[n], hk)
        if p_prev is not None:
            weighted_values(*p_prev, *work[n - 1])
        p_prev = p_cur
    filled += [f() for f in fillers[FILL_PER_STEP * len(work):]]
    weighted_values(*p_prev, *work[-1])

    def take(width):
        chunks = [filled.pop(0) for _ in range(width // MXU_COLS)]
        return jnp.concatenate(chunks, axis=1)

    gate_a = take(D_MODEL)
    gate_b = take(D_MODEL)
    z_a = take(ATTN_WIDTH)
    u_all = take(POOL_WIDTH)
    z_b = take(POOL_WIDTH)
    u_all = jnp.concatenate([jnp.where(is_first, 0.0, u_all[:POOL_LEAD]),
                             u_all[POOL_LEAD:POOL_LEAD + TILE],
                             jnp.where(is_last, 0.0, u_all[POOL_LEAD + TILE:])], axis=0)

    gated_a = (attn_scr[...] * _silu(z_a)).astype(bf16)
    y_a = jnp.dot(gated_a, w_a_ref[...], preferred_element_type=f32)
    mixed = [pooled_group(u_all[:, gi * POOL_GROUP:(gi + 1) * POOL_GROUP], gi)
             for gi in range(len(POOL_WINDOWS))]
    mixed = jnp.concatenate(mixed, axis=1) * pool_scale_ref[...]
    gated_b = (mixed * _silu(z_b)).astype(bf16)
    y_b = jnp.dot(gated_b, w_b_ref[...], preferred_element_type=f32)
    t_a = gate_a * y_a

    final_g = final_g_ref[...]
    for r in range(0, TILE, TAIL_ROWS):
        rows = slice(r, r + TAIL_ROWS)
        merged = (t_a[rows] + gate_b[rows] * y_b[rows]).astype(bf16)
        y = x_cur_ref[rows, :] + jnp.dot(merged, w_out_ref[...], preferred_element_type=f32)
        out_ref[rows, :] = _rms_norm(y, final_g)

    h_new = jnp.concatenate([normed(x_nxt_ref, r) for r in range(HALO, TILE, BLOCK)]
                            + [normed(x_after_ref, 0)], axis=0)
    h_keep = h_scr[TILE:ROWS, :]
    k_keep, v_keep = k_scr[:, :, TILE:ROWS], v_scr[:, TILE:ROWS, :]
    h_scr[0:2 * HALO, :] = h_keep
    h_scr[2 * HALO:ROWS, :] = h_new
    k_scr[:, :, 0:2 * HALO] = k_keep
    v_scr[:, 0:2 * HALO, :] = v_keep
    for r in range(0, TILE, 2 * HALO):
        store_kv(h_new[r:r + 2 * HALO], 2 * HALO + r)
    store_q(jnp.concatenate([h_keep[HALO:], h_new[:TILE - HALO]], axis=0))


def kernel(x, norm_g, w_in, attn_sink, pool_w, pool_scale, w_branch_a, w_branch_b, w_out, final_norm_g):
    batch, seq_len, d_model = x.shape
    depth = norm_g.shape[0]
    assert depth == 1, "the final RMSNorm is fused into the single layer's kernel"
    assert d_model == D_MODEL and w_in.shape[-1] == IN_WIDTH
    assert seq_len % TILE == 0 and seq_len // BLOCK >= 2 and seq_len >= 2 * POOL_EDGE_ROWS
    assert POOL_LEAD >= max(POOL_WINDOWS) // 2 and POOL_EDGE_ROWS >= max(POOL_WINDOWS) // 2
    assert TILE // BLOCK >= 2 and TILE // TAIL_ROWS >= 2
    bf16 = jnp.bfloat16
    bias = jnp.asarray(_attention_bias())
    inv_count = jnp.asarray(_pool_inv_count(seq_len))
    tiles_per_seq = seq_len // TILE
    n_steps = batch * tiles_per_seq
    blk_per_tile = TILE // BLOCK
    last_blk = n_steps * blk_per_tile - 1
    pool_rows = len(POOL_WINDOWS) * POOL_GROUP

    def resident(shape):
        return pl.BlockSpec(shape, lambda s: (0,) * len(shape), pipeline_mode=pl.Buffered(1))

    in_hbm = pl.BlockSpec(memory_space=pl.ANY)
    call = pl.pallas_call(
        functools.partial(_block_kernel, tiles_per_seq=tiles_per_seq),
        grid=(n_steps,),
        in_specs=[
            pl.BlockSpec((TILE, D_MODEL), lambda s: (s, 0)),
            pl.BlockSpec((TILE, D_MODEL), lambda s: (jnp.minimum(s + 1, n_steps - 1), 0)),
            pl.BlockSpec((BLOCK, D_MODEL), lambda s: (jnp.minimum((s + 2) * blk_per_tile, last_blk), 0)),
            resident((1, D_MODEL)),
            in_hbm,
            resident((N_KV_HEADS, GROUP * BLOCK, 1)),
            resident((3, N_KV_HEADS, GROUP * BLOCK, 3 * BLOCK)),
            resident(inv_count.shape),
            in_hbm,
            resident((1, POOL_WIDTH)),
            in_hbm,
            in_hbm,
            in_hbm,
            resident((1, D_MODEL)),
        ],
        out_specs=pl.BlockSpec((TILE, D_MODEL), lambda s: (s, 0)),
        out_shape=jax.ShapeDtypeStruct((batch * seq_len, d_model), x.dtype),
        scratch_shapes=[
            pltpu.VMEM((D_MODEL, IN_WIDTH), bf16),
            pltpu.VMEM((pool_rows, POOL_GROUP), bf16),
            pltpu.VMEM((ATTN_WIDTH, D_MODEL), bf16),
            pltpu.VMEM((POOL_WIDTH, D_MODEL), bf16),
            pltpu.VMEM((D_MODEL, D_MODEL), bf16),
            pltpu.VMEM((W_SLOTS, W_IN_CHUNK, IN_WIDTH), jnp.float32),
            pltpu.VMEM((W_SLOTS, W_SQ_CHUNK, D_MODEL), jnp.float32),
            pltpu.VMEM((W_SLOTS, W_SQ_CHUNK, POOL_GROUP), jnp.float32),
            pltpu.SemaphoreType.DMA((3, W_SLOTS)),
            pltpu.VMEM((ROWS, D_MODEL), bf16),
            pltpu.VMEM((N_KV_HEADS, 2 * HEAD_DIM, ROWS), bf16),
            pltpu.VMEM((N_KV_HEADS, ROWS, 2 * HEAD_DIM), bf16),
            pltpu.VMEM((TILE, ATTN_WIDTH), bf16),
            pltpu.VMEM((TILE, ATTN_WIDTH), jnp.float32),
        ],
        compiler_params=pltpu.CompilerParams(
            dimension_semantics=("arbitrary",),
            vmem_limit_bytes=VMEM_LIMIT_BYTES),
        name="hybrid_block",
    )

    x2d = x.reshape(batch * seq_len, d_model)
    sink_col = jnp.repeat(attn_sink[0].astype(jnp.float32) * LOG2_E, BLOCK).reshape(N_KV_HEADS, GROUP * BLOCK, 1)
    y2d = call(x2d, x2d, x2d,
               norm_g[0].reshape(1, D_MODEL), w_in[0], sink_col, bias, inv_count,
               pool_w[0].reshape(pool_rows, POOL_GROUP), pool_scale[0].reshape(1, POOL_WIDTH),
               w_branch_a[0], w_branch_b[0], w_out[0],
               final_norm_g.reshape(1, D_MODEL))
    return y2d.reshape(batch, seq_len, d_model)
```

```python
import functools

import numpy as np
import jax
import jax.numpy as jnp
from jax import lax
from jax.experimental import pallas as pl
from jax.experimental.pallas import tpu as pltpu

D_MODEL = 1024
N_Q_HEADS = 8
N_KV_HEADS = 2
GROUP = N_Q_HEADS // N_KV_HEADS
HEAD_DIM = 64
ATTN_WIDTH = N_Q_HEADS * HEAD_DIM
KV_WIDTH = N_KV_HEADS * HEAD_DIM
WINDOW = 128
BLOCK = 128
POOL_WIDTH = 512
POOL_WINDOWS = (2, 4, 8, 16)
POOL_GROUP = 128
EPS = 1e-6
NEG_INF = -1e30
LOG2_E = 1.4426950408889634

OFF_Q = 0
OFF_K = OFF_Q + ATTN_WIDTH
OFF_V = OFF_K + KV_WIDTH
OFF_ZA = OFF_V + KV_WIDTH
OFF_U = OFF_ZA + ATTN_WIDTH
OFF_ZB = OFF_U + POOL_WIDTH
OFF_GA = OFF_ZB + POOL_WIDTH
OFF_GB = OFF_GA + D_MODEL
IN_WIDTH = OFF_GB + D_MODEL

TILE = 512
HALO = BLOCK
ROWS = TILE + 2 * HALO
MXU_COLS = 256
TAIL_ROWS = 256
SCORE_LEAD = 2
FILL_PER_STEP = 1
POOL_EDGE_ROWS = 16
POOL_LEAD = 16
POOL_TAIL = 32
W_IN_CHUNK = 64
W_SQ_CHUNK = 128
W_SLOTS = 3
VMEM_LIMIT_BYTES = 56 * 1024 * 1024
VEC_ROWS = 4
VEC_NORM_G, VEC_FINAL_G, VEC_POOL_SCALE = 0, 1, 2


def _attention_bias():
    r = np.arange(BLOCK)[:, None]
    c = np.arange(3 * BLOCK)[None, :]
    dist = np.abs(r - (c - BLOCK))
    in_window = dist <= WINDOW
    slopes = np.exp2(-8.0 * np.arange(1, N_Q_HEADS + 1, dtype=np.float64) / N_Q_HEADS)
    alibi = -slopes[:, None, None] * dist[None].astype(np.float64)
    valid = [np.ones_like(c, bool), c >= BLOCK, c < 2 * BLOCK]
    out = np.stack([np.where(in_window & v, alibi * LOG2_E, NEG_INF) for v in valid])
    return out.reshape(3, N_KV_HEADS, GROUP * BLOCK, 3 * BLOCK).astype(np.float32)


def _pool_inv_count(seq_len):
    out = np.zeros((2, 2, len(POOL_WINDOWS), POOL_EDGE_ROWS, 1), np.float64)
    for gi, w in enumerate(POOL_WINDOWS):
        half = w // 2
        out[:, 0, gi] = 1.0 / w
        for end, pos in enumerate([np.arange(POOL_EDGE_ROWS), seq_len - POOL_EDGE_ROWS + np.arange(POOL_EDGE_ROWS)]):
            count = np.minimum(pos + half, seq_len) - np.maximum(pos - half, 0)
            out[end, 1, gi, :, 0] = 1.0 / count
    return np.broadcast_to(out, out.shape[:-1] + (POOL_GROUP,)).astype(np.float32)


def _rms_norm(x, g):
    ms = jnp.mean(x * x, axis=-1, keepdims=True)
    return x * lax.rsqrt(ms + EPS) * g


def _sigmoid(x):
    return 0.5 * jnp.tanh(0.5 * x) + 0.5


def _silu(x):
    return x * _sigmoid(x)


def _cast_weights(streams):
    rings = []
    for pairs, stage_ref, sem_ref in streams:
        chunk = stage_ref.shape[1]
        jobs = [(src, dst, r) for src, dst in pairs for r in range(0, dst.shape[0], chunk)]
        rings.append((jobs, stage_ref, sem_ref))

    def copy(ring, c):
        jobs, stage_ref, sem_ref = ring
        src, _, r = jobs[c]
        slot = c % stage_ref.shape[0]
        return pltpu.make_async_copy(src.at[pl.ds(r, stage_ref.shape[1]), :], stage_ref.at[slot], sem_ref.at[slot])

    for ring in rings:
        for c in range(min(ring[1].shape[0], len(ring[0]))):
            copy(ring, c).start()
    for c in range(max(len(ring[0]) for ring in rings)):
        for ring in rings:
            jobs, stage_ref, _ = ring
            if c < len(jobs):
                n_slots, chunk = stage_ref.shape[0], stage_ref.shape[1]
                _, dst, r = jobs[c]
                copy(ring, c).wait()
                dst[r:r + chunk, :] = stage_ref[c % n_slots].astype(dst.dtype)
                if c + n_slots < len(jobs):
                    copy(ring, c + n_slots).start()


def _block_kernel(x_cur_ref, x_nxt_ref, x_after_ref, vec_ref, w_in_hbm, sink_ref, bias_ref, cnt_ref,
                  pool_w_hbm, w_a_hbm, w_b_hbm, w_out_hbm,
                  out_ref, w_in_ref, pool_w_ref, w_a_ref, w_b_ref, w_out_ref, stage_in, stage_sq, stage_pool, sems,
                  h_scr, k_scr, v_scr, q_scr, attn_scr, *, tiles_per_seq):
    f32, bf16 = jnp.float32, jnp.bfloat16
    step = pl.program_id(0)
    i = lax.rem(step, tiles_per_seq)
    n_blk = TILE // BLOCK
    is_first = i == 0
    is_last = i == tiles_per_seq - 1
    norm_g = vec_ref[VEC_NORM_G:VEC_NORM_G + 1, :]
    lane = lax.broadcasted_iota(jnp.int32, (1, 2 * HEAD_DIM), 1)
    low = lane < HEAD_DIM

    def proj(h, off, width):
        return jnp.dot(h, w_in_ref[:, off:off + width], preferred_element_type=f32)

    def proj_chunks(rows, off, width, act=lambda a: a):
        return [lambda c=c: act(proj(h_scr[rows, :], off + c, MXU_COLS)) for c in range(0, width, MXU_COLS)]

    def normed(x_ref, r):
        return _rms_norm(x_ref[r:r + BLOCK, :], norm_g).astype(bf16)

    def store_kv(h_rows, dst):
        n = h_rows.shape[0]
        kv = proj(h_rows, OFF_K, 2 * KV_WIDTH)
        for scr, a in ((k_scr, kv[:, :KV_WIDTH]), (v_scr, kv[:, KV_WIDTH:])):
            swapped = pltpu.roll(a, HEAD_DIM, axis=1)
            scr[0, dst:dst + n, :] = jnp.where(low, a, swapped).astype(bf16)
            scr[1, dst:dst + n, :] = jnp.where(low, swapped, a).astype(bf16)

    def store_q(h_rows):
        q_scr[...] = (proj(h_rows, OFF_Q, ATTN_WIDTH) * (HEAD_DIM ** -0.5 * LOG2_E)).astype(bf16)

    @pl.when(step == 0)
    def _prepare_first_tile():
        _cast_weights([
            ([(w_in_hbm, w_in_ref)], stage_in, sems.at[0]),
            ([(w_out_hbm, w_out_ref), (w_a_hbm, w_a_ref), (w_b_hbm, w_b_ref)], stage_sq, sems.at[1]),
            ([(pool_w_hbm, pool_w_ref)], stage_pool, sems.at[2]),
        ])
        zeros = jnp.zeros((HALO, D_MODEL), bf16)
        h_new = jnp.concatenate([normed(x_cur_ref, r) for r in range(0, TILE, BLOCK)]
                                + [normed(x_nxt_ref, 0)], axis=0)
        h_scr[0:HALO, :] = zeros
        h_scr[HALO:ROWS, :] = h_new
        store_kv(zeros, 0)
        store_kv(h_new, HALO)
        store_q(h_new[0:TILE])

    zero = jnp.zeros((), bf16)

    def scores(hk, j):
        rows = slice(j * BLOCK, (j + 1) * BLOCK)
        pieces = []
        for pair in range(GROUP // 2):
            col = (hk * GROUP // 2 + pair) * 2 * HEAD_DIM
            qp = q_scr[rows, col:col + 2 * HEAD_DIM]
            pieces.append(jnp.where(low, qp, zero))
            pieces.append(jnp.where(low, zero, qp))
        lhs = jnp.concatenate(pieces, axis=0)
        kwin = k_scr[hk, j * BLOCK:j * BLOCK + 3 * BLOCK, :]
        s = lax.dot_general(lhs, kwin, (((1,), (1,)), ((), ())), preferred_element_type=f32)
        if j == 0:
            variant = jnp.where(is_first, 1, 0)
        elif j == n_blk - 1:
            variant = jnp.where(is_last, 2, 0)
        else:
            variant = 0
        return s + bias_ref[variant, hk]

    def softmax(s, hk):
        sink_col = sink_ref[hk]
        m = jnp.maximum(jnp.max(s, axis=-1, keepdims=True), sink_col)
        p = jnp.exp2(s - m)
        denom = jnp.sum(p, axis=-1, keepdims=True) + jnp.exp2(sink_col - m)
        return p.astype(bf16), denom

    def weighted_values(p, denom, hk, j):
        rows = slice(j * BLOCK, (j + 1) * BLOCK)
        vwin = v_scr[hk, j * BLOCK:j * BLOCK + 3 * BLOCK, :]
        o = jnp.dot(p, vwin, preferred_element_type=f32) / denom
        for pair in range(GROUP // 2):
            even = o[(2 * pair) * BLOCK:(2 * pair + 1) * BLOCK]
            odd = o[(2 * pair + 1) * BLOCK:(2 * pair + 2) * BLOCK]
            col = (hk * GROUP // 2 + pair) * 2 * HEAD_DIM
            attn_scr[rows, col:col + 2 * HEAD_DIM] = jnp.where(low, even, odd)

    def pooled_group(u, gi):
        w = POOL_WINDOWS[gi]
        r0, n = POOL_LEAD, TILE
        if w == 2:
            wsum = u[r0 - 1:r0 - 1 + n] + u[r0:r0 + n]
        else:
            c = u[0:n + 32] + u[1:n + 33]
            if w == 4:
                wsum = c[r0 - 2:r0 - 2 + n] + c[r0:r0 + n]
            else:
                e = c[0:n + 24] + c[2:n + 26]
                if w == 8:
                    wsum = e[r0 - 4:r0 - 4 + n] + e[r0:r0 + n]
                else:
                    f = e[0:n + 16] + e[4:n + 20]
                    wsum = f[r0 - 8:r0 - 8 + n] + f[r0:r0 + n]
        edge = POOL_EDGE_ROWS
        head = wsum[:edge] * cnt_ref[0, jnp.where(is_first, 1, 0), gi]
        body = wsum[edge:n - edge] * (1.0 / w)
        tail = wsum[n - edge:] * cnt_ref[1, jnp.where(is_last, 1, 0), gi]
        pooled = jnp.concatenate([head, body, tail], axis=0) - u[r0:r0 + n]
        pool_w = pool_w_ref[gi * POOL_GROUP:(gi + 1) * POOL_GROUP, :]
        return jnp.dot(pooled.astype(bf16), pool_w, preferred_element_type=f32)

    centre = slice(HALO, HALO + TILE)
    pool_rows = slice(HALO - POOL_LEAD, HALO + TILE + POOL_TAIL)
    work = [(hk, j) for j in range(n_blk) for hk in range(N_KV_HEADS)]
    fillers = (proj_chunks(centre, OFF_GA, D_MODEL, _sigmoid)
               + proj_chunks(centre, OFF_GB, D_MODEL, _sigmoid)
               + proj_chunks(centre, OFF_ZA, ATTN_WIDTH)
               + proj_chunks(pool_rows, OFF_U, POOL_WIDTH)
               + proj_chunks(centre, OFF_ZB, POOL_WIDTH))
    filled = []
    s_list = [scores(*work[n]) for n in range(SCORE_LEAD)]
    p_prev = None
    for n, (hk, j) in enumerate(work):
        filled += [f() for f in fillers[FILL_PER_STEP * n:FILL_PER_STEP * (n + 1)]]
        if n + SCORE_LEAD < len(work):
            s_list.append(scores(*work[n + SCORE_LEAD]))
        p_cur = softmax(s_list[n], hk)
        if p_prev is not None:
            weighted_values(*p_prev, *work[n - 1])
        p_prev = p_cur
    filled += [f() for f in fillers[FILL_PER_STEP * len(work):]]
    weighted_values(*p_prev, *work[-1])

    def take(width):
        chunks = [filled.pop(0) for _ in range(width // MXU_COLS)]
        return jnp.concatenate(chunks, axis=1)

    gate_a = take(D_MODEL)
    gate_b = take(D_MODEL)
    z_a = take(ATTN_WIDTH)
    u_all = take(POOL_WIDTH)
    z_b = take(POOL_WIDTH)
    u_all = jnp.concatenate([jnp.where(is_first, 0.0, u_all[:POOL_LEAD]),
                             u_all[POOL_LEAD:POOL_LEAD + TILE],
                             jnp.where(is_last, 0.0, u_all[POOL_LEAD + TILE:])], axis=0)

    gated_a = (attn_scr[...] * _silu(z_a)).astype(bf16)
    y_a = jnp.dot(gated_a, w_a_ref[...], preferred_element_type=f32)
    mixed = [pooled_group(u_all[:, gi * POOL_GROUP:(gi + 1) * POOL_GROUP], gi)
             for gi in range(len(POOL_WINDOWS))]
    mixed = jnp.concatenate(mixed, axis=1) * vec_ref[VEC_POOL_SCALE:VEC_POOL_SCALE + 1, 0:POOL_WIDTH]
    gated_b = (mixed * _silu(z_b)).astype(bf16)
    y_b = jnp.dot(gated_b, w_b_ref[...], preferred_element_type=f32)
    t_a = gate_a * y_a

    final_g = vec_ref[VEC_FINAL_G:VEC_FINAL_G + 1, :]
    for r in range(0, TILE, TAIL_ROWS):
        rows = slice(r, r + TAIL_ROWS)
        merged = (t_a[rows] + gate_b[rows] * y_b[rows]).astype(bf16)
        y = x_cur_ref[rows, :] + jnp.dot(merged, w_out_ref[...], preferred_element_type=f32)
        out_ref[rows, :] = _rms_norm(y, final_g)

    h_new = jnp.concatenate([normed(x_nxt_ref, r) for r in range(HALO, TILE, BLOCK)]
                            + [normed(x_after_ref, 0)], axis=0)
    h_keep = h_scr[TILE:ROWS, :]
    k_keep, v_keep = k_scr[:, TILE:ROWS, :], v_scr[:, TILE:ROWS, :]
    h_scr[0:2 * HALO, :] = h_keep
    h_scr[2 * HALO:ROWS, :] = h_new
    k_scr[:, 0:2 * HALO, :] = k_keep
    v_scr[:, 0:2 * HALO, :] = v_keep
    for r in range(0, TILE, 2 * HALO):
        store_kv(h_new[r:r + 2 * HALO], 2 * HALO + r)
    store_q(jnp.concatenate([h_keep[HALO:], h_new[:TILE - HALO]], axis=0))


def kernel(x, norm_g, w_in, attn_sink, pool_w, pool_scale, w_branch_a, w_branch_b, w_out, final_norm_g):
    batch, seq_len, d_model = x.shape
    depth = norm_g.shape[0]
    assert depth == 1, "the final RMSNorm is fused into the single layer's kernel"
    assert d_model == D_MODEL and w_in.shape[-1] == IN_WIDTH
    assert seq_len % TILE == 0 and seq_len // BLOCK >= 2 and seq_len >= 2 * POOL_EDGE_ROWS
    assert POOL_LEAD >= max(POOL_WINDOWS) // 2 and POOL_EDGE_ROWS >= max(POOL_WINDOWS) // 2
    assert TILE // BLOCK >= 2 and TILE // TAIL_ROWS >= 2
    bf16 = jnp.bfloat16
    bias = jnp.asarray(_attention_bias())
    inv_count = jnp.asarray(_pool_inv_count(seq_len))
    tiles_per_seq = seq_len // TILE
    n_steps = batch * tiles_per_seq
    blk_per_tile = TILE // BLOCK
    last_blk = n_steps * blk_per_tile - 1
    pool_rows = len(POOL_WINDOWS) * POOL_GROUP

    def resident(shape):
        return pl.BlockSpec(shape, lambda s: (0,) * len(shape), pipeline_mode=pl.Buffered(1))

    in_hbm = pl.BlockSpec(memory_space=pl.ANY)
    call = pl.pallas_call(
        functools.partial(_block_kernel, tiles_per_seq=tiles_per_seq),
        grid=(n_steps,),
        in_specs=[
            pl.BlockSpec((TILE, D_MODEL), lambda s: (s, 0)),
            pl.BlockSpec((TILE, D_MODEL), lambda s: (jnp.minimum(s + 1, n_steps - 1), 0)),
            pl.BlockSpec((BLOCK, D_MODEL), lambda s: (jnp.minimum((s + 2) * blk_per_tile, last_blk), 0)),
            resident((VEC_ROWS, D_MODEL)),
            in_hbm,
            resident((N_KV_HEADS, GROUP * BLOCK, 1)),
            resident((3, N_KV_HEADS, GROUP * BLOCK, 3 * BLOCK)),
            resident(inv_count.shape),
            in_hbm,
            in_hbm,
            in_hbm,
            in_hbm,
        ],
        out_specs=pl.BlockSpec((TILE, D_MODEL), lambda s: (s, 0)),
        out_shape=jax.ShapeDtypeStruct((batch * seq_len, d_model), x.dtype),
        scratch_shapes=[
            pltpu.VMEM((D_MODEL, IN_WIDTH), bf16),
            pltpu.VMEM((pool_rows, POOL_GROUP), bf16),
            pltpu.VMEM((ATTN_WIDTH, D_MODEL), bf16),
            pltpu.VMEM((POOL_WIDTH, D_MODEL), bf16),
            pltpu.VMEM((D_MODEL, D_MODEL), bf16),
            pltpu.VMEM((W_SLOTS, W_IN_CHUNK, IN_WIDTH), jnp.float32),
            pltpu.VMEM((W_SLOTS, W_SQ_CHUNK, D_MODEL), jnp.float32),
            pltpu.VMEM((W_SLOTS, W_SQ_CHUNK, POOL_GROUP), jnp.float32),
            pltpu.SemaphoreType.DMA((3, W_SLOTS)),
            pltpu.VMEM((ROWS, D_MODEL), bf16),
            pltpu.VMEM((N_KV_HEADS, ROWS, 2 * HEAD_DIM), bf16),
            pltpu.VMEM((N_KV_HEADS, ROWS, 2 * HEAD_DIM), bf16),
            pltpu.VMEM((TILE, ATTN_WIDTH), bf16),
            pltpu.VMEM((TILE, ATTN_WIDTH), jnp.float32),
        ],
        compiler_params=pltpu.CompilerParams(
            dimension_semantics=("arbitrary",),
            vmem_limit_bytes=VMEM_LIMIT_BYTES),
        name="hybrid_block",
    )

    x2d = x.reshape(batch * seq_len, d_model)
    sink_col = jnp.repeat(attn_sink[0].astype(jnp.float32) * LOG2_E, BLOCK).reshape(N_KV_HEADS, GROUP * BLOCK, 1)
    vecs = jnp.zeros((VEC_ROWS, D_MODEL), jnp.float32)
    vecs = vecs.at[VEC_NORM_G].set(norm_g[0]).at[VEC_FINAL_G].set(final_norm_g)
    vecs = vecs.at[VEC_POOL_SCALE, :POOL_WIDTH].set(pool_scale[0])
    y2d = call(x2d, x2d, x2d,
               vecs, w_in[0], sink_col, bias, inv_count,
               pool_w[0].reshape(pool_rows, POOL_GROUP),
               w_branch_a[0], w_branch_b[0], w_out[0])
    return y2d.reshape(batch, seq_len, d_model)
```

```python
import functools

import numpy as np
import jax
import jax.numpy as jnp
from jax import lax
from jax.experimental import pallas as pl
from jax.experimental.pallas import tpu as pltpu

D_MODEL = 1024
N_Q_HEADS = 8
N_KV_HEADS = 2
GROUP = N_Q_HEADS // N_KV_HEADS
HEAD_DIM = 64
ATTN_WIDTH = N_Q_HEADS * HEAD_DIM
KV_WIDTH = N_KV_HEADS * HEAD_DIM
WINDOW = 128
BLOCK = 128
POOL_WIDTH = 512
POOL_WINDOWS = (2, 4, 8, 16)
POOL_GROUP = 128
EPS = 1e-6
NEG_INF = -1e30
LOG2_E = 1.4426950408889634

OFF_Q = 0
OFF_K = OFF_Q + ATTN_WIDTH
OFF_V = OFF_K + KV_WIDTH
OFF_ZA = OFF_V + KV_WIDTH
OFF_U = OFF_ZA + ATTN_WIDTH
OFF_ZB = OFF_U + POOL_WIDTH
OFF_GA = OFF_ZB + POOL_WIDTH
OFF_GB = OFF_GA + D_MODEL
IN_WIDTH = OFF_GB + D_MODEL

TILE = 512
HALO = BLOCK
ROWS = TILE + 2 * HALO
MXU_COLS = 256
TAIL_ROWS = 256
SCORE_LEAD = 2
FILL_PER_STEP = 1
POOL_EDGE_ROWS = 16
POOL_LEAD = 16
POOL_TAIL = 32
W_IN_CHUNK = 64
W_SQ_CHUNK = 128
W_SLOTS = 3
VMEM_LIMIT_BYTES = 56 * 1024 * 1024
VEC_ROWS = 4
VEC_NORM_G, VEC_FINAL_G, VEC_POOL_SCALE = 0, 1, 2


def _attention_bias():
    r = np.arange(BLOCK)[:, None]
    c = np.arange(3 * BLOCK)[None, :]
    dist = np.abs(r - (c - BLOCK))
    in_window = dist <= WINDOW
    slopes = np.exp2(-8.0 * np.arange(1, N_Q_HEADS + 1, dtype=np.float64) / N_Q_HEADS)
    alibi = -slopes[:, None, None] * dist[None].astype(np.float64)
    out = np.where(in_window, alibi * LOG2_E, NEG_INF)
    return out.reshape(N_KV_HEADS, GROUP * BLOCK, 3 * BLOCK).astype(np.float32)


def _pool_inv_count(seq_len):
    out = np.zeros((2, 2, len(POOL_WINDOWS), POOL_EDGE_ROWS, 1), np.float64)
    for gi, w in enumerate(POOL_WINDOWS):
        half = w // 2
        out[:, 0, gi] = 1.0 / w
        for end, pos in enumerate([np.arange(POOL_EDGE_ROWS), seq_len - POOL_EDGE_ROWS + np.arange(POOL_EDGE_ROWS)]):
            count = np.minimum(pos + half, seq_len) - np.maximum(pos - half, 0)
            out[end, 1, gi, :, 0] = 1.0 / count
    return np.broadcast_to(out, out.shape[:-1] + (POOL_GROUP,)).astype(np.float32)


def _rms_norm(x, g):
    ms = jnp.mean(x * x, axis=-1, keepdims=True)
    return x * lax.rsqrt(ms + EPS) * g


def _sigmoid(x):
    return 0.5 * jnp.tanh(0.5 * x) + 0.5


def _silu(x):
    return x * _sigmoid(x)


def _cast_weights(streams):
    rings = []
    for pairs, stage_ref, sem_ref in streams:
        chunk = stage_ref.shape[1]
        jobs = [(src, dst, r) for src, dst in pairs for r in range(0, dst.shape[0], chunk)]
        rings.append((jobs, stage_ref, sem_ref))

    def copy(ring, c):
        jobs, stage_ref, sem_ref = ring
        src, _, r = jobs[c]
        slot = c % stage_ref.shape[0]
        return pltpu.make_async_copy(src.at[pl.ds(r, stage_ref.shape[1]), :], stage_ref.at[slot], sem_ref.at[slot])

    for ring in rings:
        for c in range(min(ring[1].shape[0], len(ring[0]))):
            copy(ring, c).start()
    for c in range(max(len(ring[0]) for ring in rings)):
        for ring in rings:
            jobs, stage_ref, _ = ring
            if c < len(jobs):
                n_slots, chunk = stage_ref.shape[0], stage_ref.shape[1]
                _, dst, r = jobs[c]
                copy(ring, c).wait()
                dst[r:r + chunk, :] = stage_ref[c % n_slots].astype(dst.dtype)
                if c + n_slots < len(jobs):
                    copy(ring, c + n_slots).start()


def _block_kernel(x_cur_ref, x_nxt_ref, x_after_ref, vec_ref, w_in_hbm, sink_ref, bias_ref, cnt_ref,
                  pool_w_hbm, w_a_hbm, w_b_hbm, w_out_hbm,
                  out_ref, w_in_ref, pool_w_ref, w_a_ref, w_b_ref, w_out_ref, stage_in, stage_sq, stage_pool, sems,
                  h_scr, k_scr, v_scr, q_scr, attn_scr, *, tiles_per_seq):
    f32, bf16 = jnp.float32, jnp.bfloat16
    step = pl.program_id(0)
    i = lax.rem(step, tiles_per_seq)
    n_blk = TILE // BLOCK
    is_first = i == 0
    is_last = i == tiles_per_seq - 1
    norm_g = vec_ref[VEC_NORM_G:VEC_NORM_G + 1, :]
    lane = lax.broadcasted_iota(jnp.int32, (1, 2 * HEAD_DIM), 1)
    low = lane < HEAD_DIM

    def proj(h, off, width):
        return jnp.dot(h, w_in_ref[:, off:off + width], preferred_element_type=f32)

    def proj_chunks(rows, off, width, act=lambda a: a):
        return [lambda c=c: act(proj(h_scr[rows, :], off + c, MXU_COLS)) for c in range(0, width, MXU_COLS)]

    def normed(x_ref, r):
        return _rms_norm(x_ref[r:r + BLOCK, :], norm_g).astype(bf16)

    def store_kv(h_rows, dst):
        n = h_rows.shape[0]
        kv = proj(h_rows, OFF_K, 2 * KV_WIDTH)
        for scr, a in ((k_scr, kv[:, :KV_WIDTH]), (v_scr, kv[:, KV_WIDTH:])):
            swapped = pltpu.roll(a, HEAD_DIM, axis=1)
            scr[0, dst:dst + n, :] = jnp.where(low, a, swapped).astype(bf16)
            scr[1, dst:dst + n, :] = jnp.where(low, swapped, a).astype(bf16)

    def store_q(h_rows):
        q_scr[...] = (proj(h_rows, OFF_Q, ATTN_WIDTH) * (HEAD_DIM ** -0.5 * LOG2_E)).astype(bf16)

    @pl.when(step == 0)
    def _prepare_first_tile():
        _cast_weights([
            ([(w_in_hbm, w_in_ref)], stage_in, sems.at[0]),
            ([(w_out_hbm, w_out_ref), (w_a_hbm, w_a_ref), (w_b_hbm, w_b_ref)], stage_sq, sems.at[1]),
            ([(pool_w_hbm, pool_w_ref)], stage_pool, sems.at[2]),
        ])
        zeros = jnp.zeros((HALO, D_MODEL), bf16)
        h_new = jnp.concatenate([normed(x_cur_ref, r) for r in range(0, TILE, BLOCK)]
                                + [normed(x_nxt_ref, 0)], axis=0)
        h_scr[0:HALO, :] = zeros
        h_scr[HALO:ROWS, :] = h_new
        store_kv(zeros, 0)
        store_kv(h_new, HALO)
        store_q(h_new[0:TILE])

    zero = jnp.zeros((), bf16)
    key_col = lax.broadcasted_iota(jnp.int32, (1, 3 * BLOCK), 1)

    def scores(hk, j):
        rows = slice(j * BLOCK, (j + 1) * BLOCK)
        pieces = []
        for pair in range(GROUP // 2):
            col = (hk * GROUP // 2 + pair) * 2 * HEAD_DIM
            qp = q_scr[rows, col:col + 2 * HEAD_DIM]
            pieces.append(jnp.where(low, qp, zero))
            pieces.append(jnp.where(low, zero, qp))
        lhs = jnp.concatenate(pieces, axis=0)
        kwin = k_scr[hk, j * BLOCK:j * BLOCK + 3 * BLOCK, :]
        s = lax.dot_general(lhs, kwin, (((1,), (1,)), ((), ())), preferred_element_type=f32)
        s = s + bias_ref[hk]
        if j == 0:
            s = s + jnp.where(is_first & (key_col < BLOCK), NEG_INF, 0.0)
        elif j == n_blk - 1:
            s = s + jnp.where(is_last & (key_col >= 2 * BLOCK), NEG_INF, 0.0)
        return s

    def softmax(s, hk):
        sink_col = sink_ref[hk]
        m = jnp.maximum(jnp.max(s, axis=-1, keepdims=True), sink_col)
        p = jnp.exp2(s - m)
        denom = jnp.sum(p, axis=-1, keepdims=True) + jnp.exp2(sink_col - m)
        return p.astype(bf16), denom

    def weighted_values(p, denom, hk, j):
        rows = slice(j * BLOCK, (j + 1) * BLOCK)
        vwin = v_scr[hk, j * BLOCK:j * BLOCK + 3 * BLOCK, :]
        o = jnp.dot(p, vwin, preferred_element_type=f32) / denom
        for pair in range(GROUP // 2):
            even = o[(2 * pair) * BLOCK:(2 * pair + 1) * BLOCK]
            odd = o[(2 * pair + 1) * BLOCK:(2 * pair + 2) * BLOCK]
            col = (hk * GROUP // 2 + pair) * 2 * HEAD_DIM
            attn_scr[rows, col:col + 2 * HEAD_DIM] = jnp.where(low, even, odd)

    def pooled_group(u, gi):
        w = POOL_WINDOWS[gi]
        r0, n = POOL_LEAD, TILE
        if w == 2:
            wsum = u[r0 - 1:r0 - 1 + n] + u[r0:r0 + n]
        else:
            c = u[0:n + 32] + u[1:n + 33]
            if w == 4:
                wsum = c[r0 - 2:r0 - 2 + n] + c[r0:r0 + n]
            else:
                e = c[0:n + 24] + c[2:n + 26]
                if w == 8:
                    wsum = e[r0 - 4:r0 - 4 + n] + e[r0:r0 + n]
                else:
                    f = e[0:n + 16] + e[4:n + 20]
                    wsum = f[r0 - 8:r0 - 8 + n] + f[r0:r0 + n]
        edge = POOL_EDGE_ROWS
        head = wsum[:edge] * cnt_ref[0, jnp.where(is_first, 1, 0), gi]
        body = wsum[edge:n - edge] * (1.0 / w)
        tail = wsum[n - edge:] * cnt_ref[1, jnp.where(is_last, 1, 0), gi]
        pooled = jnp.concatenate([head, body, tail], axis=0) - u[r0:r0 + n]
        pool_w = pool_w_ref[gi * POOL_GROUP:(gi + 1) * POOL_GROUP, :]
        return jnp.dot(pooled.astype(bf16), pool_w, preferred_element_type=f32)

    centre = slice(HALO, HALO + TILE)
    pool_rows = slice(HALO - POOL_LEAD, HALO + TILE + POOL_TAIL)
    work = [(hk, j) for j in range(n_blk) for hk in range(N_KV_HEADS)]
    fillers = (proj_chunks(centre, OFF_GA, D_MODEL, _sigmoid)
               + proj_chunks(centre, OFF_GB, D_MODEL, _sigmoid)
               + proj_chunks(centre, OFF_ZA, ATTN_WIDTH)
               + proj_chunks(pool_rows, OFF_U, POOL_WIDTH)
               + proj_chunks(centre, OFF_ZB, POOL_WIDTH))
    filled = []
    s_list = [scores(*work[n]) for n in range(SCORE_LEAD)]
    p_prev = None
    for n, (hk, j) in enumerate(work):
        filled += [f() for f in fillers[FILL_PER_STEP * n:FILL_PER_STEP * (n + 1)]]
        if n + SCORE_LEAD < len(work):
            s_list.append(scores(*work[n + SCORE_LEAD]))
        p_cur = softmax(s_list[n], hk)
        if p_prev is not None:
            weighted_values(*p_prev, *work[n - 1])
        p_prev = p_cur
    filled += [f() for f in fillers[FILL_PER_STEP * len(work):]]
    weighted_values(*p_prev, *work[-1])

    def take(width):
        chunks = [filled.pop(0) for _ in range(width // MXU_COLS)]
        return jnp.concatenate(chunks, axis=1)

    gate_a = take(D_MODEL)
    gate_b = take(D_MODEL)
    z_a = take(ATTN_WIDTH)
    u_all = take(POOL_WIDTH)
    z_b = take(POOL_WIDTH)
    u_all = jnp.concatenate([jnp.where(is_first, 0.0, u_all[:POOL_LEAD]),
                             u_all[POOL_LEAD:POOL_LEAD + TILE],
                             jnp.where(is_last, 0.0, u_all[POOL_LEAD + TILE:])], axis=0)

    gated_a = (attn_scr[...] * _silu(z_a)).astype(bf16)
    y_a = jnp.dot(gated_a, w_a_ref[...], preferred_element_type=f32)
    mixed = [pooled_group(u_all[:, gi * POOL_GROUP:(gi + 1) * POOL_GROUP], gi)
             for gi in range(len(POOL_WINDOWS))]
    mixed = jnp.concatenate(mixed, axis=1) * vec_ref[VEC_POOL_SCALE:VEC_POOL_SCALE + 1, 0:POOL_WIDTH]
    gated_b = (mixed * _silu(z_b)).astype(bf16)
    y_b = jnp.dot(gated_b, w_b_ref[...], preferred_element_type=f32)
    t_a = gate_a * y_a

    final_g = vec_ref[VEC_FINAL_G:VEC_FINAL_G + 1, :]
    for r in range(0, TILE, TAIL_ROWS):
        rows = slice(r, r + TAIL_ROWS)
        merged = (t_a[rows] + gate_b[rows] * y_b[rows]).astype(bf16)
        y = x_cur_ref[rows, :] + jnp.dot(merged, w_out_ref[...], preferred_element_type=f32)
        out_ref[rows, :] = _rms_norm(y, final_g)

    h_new = jnp.concatenate([normed(x_nxt_ref, r) for r in range(HALO, TILE, BLOCK)]
                            + [normed(x_after_ref, 0)], axis=0)
    h_keep = h_scr[TILE:ROWS, :]
    k_keep, v_keep = k_scr[:, TILE:ROWS, :], v_scr[:, TILE:ROWS, :]
    h_scr[0:2 * HALO, :] = h_keep
    h_scr[2 * HALO:ROWS, :] = h_new
    k_scr[:, 0:2 * HALO, :] = k_keep
    v_scr[:, 0:2 * HALO, :] = v_keep
    for r in range(0, TILE, 2 * HALO):
        store_kv(h_new[r:r + 2 * HALO], 2 * HALO + r)
    store_q(jnp.concatenate([h_keep[HALO:], h_new[:TILE - HALO]], axis=0))


def kernel(x, norm_g, w_in, attn_sink, pool_w, pool_scale, w_branch_a, w_branch_b, w_out, final_norm_g):
    batch, seq_len, d_model = x.shape
    depth = norm_g.shape[0]
    assert depth == 1, "the final RMSNorm is fused into the single layer's kernel"
    assert d_model == D_MODEL and w_in.shape[-1] == IN_WIDTH
    assert seq_len % TILE == 0 and seq_len // BLOCK >= 2 and seq_len >= 2 * POOL_EDGE_ROWS
    assert POOL_LEAD >= max(POOL_WINDOWS) // 2 and POOL_EDGE_ROWS >= max(POOL_WINDOWS) // 2
    assert TILE // BLOCK >= 2 and TILE // TAIL_ROWS >= 2
    bf16 = jnp.bfloat16
    bias = jnp.asarray(_attention_bias())
    inv_count = jnp.asarray(_pool_inv_count(seq_len))
    tiles_per_seq = seq_len // TILE
    n_steps = batch * tiles_per_seq
    blk_per_tile = TILE // BLOCK
    last_blk = n_steps * blk_per_tile - 1
    pool_rows = len(POOL_WINDOWS) * POOL_GROUP

    def resident(shape):
        return pl.BlockSpec(shape, lambda s: (0,) * len(shape), pipeline_mode=pl.Buffered(1))

    in_hbm = pl.BlockSpec(memory_space=pl.ANY)
    call = pl.pallas_call(
        functools.partial(_block_kernel, tiles_per_seq=tiles_per_seq),
        grid=(n_steps,),
        in_specs=[
            pl.BlockSpec((TILE, D_MODEL), lambda s: (s, 0)),
            pl.BlockSpec((TILE, D_MODEL), lambda s: (jnp.minimum(s + 1, n_steps - 1), 0)),
            pl.BlockSpec((BLOCK, D_MODEL), lambda s: (jnp.minimum((s + 2) * blk_per_tile, last_blk), 0)),
            resident((VEC_ROWS, D_MODEL)),
            in_hbm,
            resident((N_KV_HEADS, GROUP * BLOCK, 1)),
            resident((N_KV_HEADS, GROUP * BLOCK, 3 * BLOCK)),
            resident(inv_count.shape),
            in_hbm,
            in_hbm,
            in_hbm,
            in_hbm,
        ],
        out_specs=pl.BlockSpec((TILE, D_MODEL), lambda s: (s, 0)),
        out_shape=jax.ShapeDtypeStruct((batch * seq_len, d_model), x.dtype),
        scratch_shapes=[
            pltpu.VMEM((D_MODEL, IN_WIDTH), bf16),
            pltpu.VMEM((pool_rows, POOL_GROUP), bf16),
            pltpu.VMEM((ATTN_WIDTH, D_MODEL), bf16),
            pltpu.VMEM((POOL_WIDTH, D_MODEL), bf16),
            pltpu.VMEM((D_MODEL, D_MODEL), bf16),
            pltpu.VMEM((W_SLOTS, W_IN_CHUNK, IN_WIDTH), jnp.float32),
            pltpu.VMEM((W_SLOTS, W_SQ_CHUNK, D_MODEL), jnp.float32),
            pltpu.VMEM((W_SLOTS, W_SQ_CHUNK, POOL_GROUP), jnp.float32),
            pltpu.SemaphoreType.DMA((3, W_SLOTS)),
            pltpu.VMEM((ROWS, D_MODEL), bf16),
            pltpu.VMEM((N_KV_HEADS, ROWS, 2 * HEAD_DIM), bf16),
            pltpu.VMEM((N_KV_HEADS, ROWS, 2 * HEAD_DIM), bf16),
            pltpu.VMEM((TILE, ATTN_WIDTH), bf16),
            pltpu.VMEM((TILE, ATTN_WIDTH), jnp.float32),
        ],
        compiler_params=pltpu.CompilerParams(
            dimension_semantics=("arbitrary",),
            vmem_limit_bytes=VMEM_LIMIT_BYTES),
        name="hybrid_block",
    )

    x2d = x.reshape(batch * seq_len, d_model)
    sink_col = jnp.repeat(attn_sink[0].astype(jnp.float32) * LOG2_E, BLOCK).reshape(N_KV_HEADS, GROUP * BLOCK, 1)
    vec_rows = {VEC_NORM_G: norm_g[0], VEC_FINAL_G: final_norm_g,
                VEC_POOL_SCALE: jnp.pad(pool_scale[0], (0, D_MODEL - POOL_WIDTH))}
    vecs = jnp.stack([vec_rows.get(r, jnp.zeros((D_MODEL,), jnp.float32)) for r in range(VEC_ROWS)]).astype(jnp.float32)
    y2d = call(x2d, x2d, x2d,
               vecs, w_in[0], sink_col, bias, inv_count,
               pool_w[0].reshape(pool_rows, POOL_GROUP),
               w_branch_a[0], w_branch_b[0], w_out[0])
    return y2d.reshape(batch, seq_len, d_model)
```

```python
import functools

import numpy as np
import jax
import jax.numpy as jnp
from jax import lax
from jax.experimental import pallas as pl
from jax.experimental.pallas import tpu as pltpu

D_MODEL = 1024
N_Q_HEADS = 8
N_KV_HEADS = 2
GROUP = N_Q_HEADS // N_KV_HEADS
HEAD_DIM = 64
ATTN_WIDTH = N_Q_HEADS * HEAD_DIM
KV_WIDTH = N_KV_HEADS * HEAD_DIM
WINDOW = 128
BLOCK = 128
POOL_WIDTH = 512
POOL_WINDOWS = (2, 4, 8, 16)
POOL_GROUP = 128
EPS = 1e-6
NEG_INF = -1e30
LOG2_E = 1.4426950408889634

OFF_Q = 0
OFF_K = OFF_Q + ATTN_WIDTH
OFF_V = OFF_K + KV_WIDTH
OFF_ZA = OFF_V + KV_WIDTH
OFF_U = OFF_ZA + ATTN_WIDTH
OFF_ZB = OFF_U + POOL_WIDTH
OFF_GA = OFF_ZB + POOL_WIDTH
OFF_GB = OFF_GA + D_MODEL
IN_WIDTH = OFF_GB + D_MODEL

TILE = 512
HALO = BLOCK
ROWS = TILE + 2 * HALO
MXU_COLS = 256
TAIL_ROWS = 256
SCORE_LEAD = 2
FILL_PER_STEP = 1
POOL_EDGE_ROWS = 16
POOL_LEAD = 16
POOL_TAIL = 32
W_IN_CHUNK = 64
W_SQ_CHUNK = 128
W_SLOTS = 3
VMEM_LIMIT_BYTES = 56 * 1024 * 1024
VEC_ROWS = 4
VEC_NORM_G, VEC_FINAL_G, VEC_POOL_SCALE = 0, 1, 2


def _attention_bias():
    r = np.arange(BLOCK)[:, None]
    c = np.arange(3 * BLOCK)[None, :]
    dist = np.abs(r - (c - BLOCK))
    in_window = dist <= WINDOW
    slopes = np.exp2(-8.0 * np.arange(1, N_Q_HEADS + 1, dtype=np.float64) / N_Q_HEADS)
    alibi = -slopes[:, None, None] * dist[None].astype(np.float64)
    out = np.where(in_window, alibi * LOG2_E, NEG_INF)
    return out.reshape(N_KV_HEADS, GROUP * BLOCK, 3 * BLOCK).astype(np.float32)


def _pool_inv_count(seq_len):
    out = np.zeros((2, 2, len(POOL_WINDOWS), POOL_EDGE_ROWS, 1), np.float64)
    for gi, w in enumerate(POOL_WINDOWS):
        half = w // 2
        out[:, 0, gi] = 1.0 / w
        for end, pos in enumerate([np.arange(POOL_EDGE_ROWS), seq_len - POOL_EDGE_ROWS + np.arange(POOL_EDGE_ROWS)]):
            count = np.minimum(pos + half, seq_len) - np.maximum(pos - half, 0)
            out[end, 1, gi, :, 0] = 1.0 / count
    return np.broadcast_to(out, out.shape[:-1] + (POOL_GROUP,)).astype(np.float32)


def _rms_norm(x, g):
    ms = jnp.mean(x * x, axis=-1, keepdims=True)
    return x * lax.rsqrt(ms + EPS) * g


def _sigmoid(x):
    return 0.5 * jnp.tanh(0.5 * x) + 0.5


def _silu(x):
    return x * _sigmoid(x)


def _cast_weights(streams):
    rings = []
    for pairs, stage_ref, sem_ref in streams:
        chunk = stage_ref.shape[1]
        jobs = [(src, dst, r) for src, dst in pairs for r in range(0, dst.shape[0], chunk)]
        rings.append((jobs, stage_ref, sem_ref))

    def copy(ring, c):
        jobs, stage_ref, sem_ref = ring
        src, _, r = jobs[c]
        slot = c % stage_ref.shape[0]
        return pltpu.make_async_copy(src.at[pl.ds(r, stage_ref.shape[1]), :], stage_ref.at[slot], sem_ref.at[slot])

    for ring in rings:
        for c in range(min(ring[1].shape[0], len(ring[0]))):
            copy(ring, c).start()
    for c in range(max(len(ring[0]) for ring in rings)):
        for ring in rings:
            jobs, stage_ref, _ = ring
            if c < len(jobs):
                n_slots, chunk = stage_ref.shape[0], stage_ref.shape[1]
                _, dst, r = jobs[c]
                copy(ring, c).wait()
                dst[r:r + chunk, :] = stage_ref[c % n_slots].astype(dst.dtype)
                if c + n_slots < len(jobs):
                    copy(ring, c + n_slots).start()


def _block_kernel(x_cur_ref, x_nxt_ref, x_after_ref, norm_g_hbm, w_in_hbm, sink_smem, bias_ref, cnt_ref,
                  pool_w_hbm, pool_scale_hbm, w_a_hbm, w_b_hbm, w_out_hbm, final_g_hbm,
                  out_ref, w_in_ref, pool_w_ref, w_a_ref, w_b_ref, w_out_ref, stage_in, stage_sq, stage_pool, sems,
                  vec_ref, sink_ref,
                  h_scr, k_scr, v_scr, q_scr, attn_scr, *, tiles_per_seq):
    f32, bf16 = jnp.float32, jnp.bfloat16
    step = pl.program_id(0)
    i = lax.rem(step, tiles_per_seq)
    n_blk = TILE // BLOCK
    is_first = i == 0
    is_last = i == tiles_per_seq - 1
    lane = lax.broadcasted_iota(jnp.int32, (1, 2 * HEAD_DIM), 1)
    low = lane < HEAD_DIM

    def proj(h, off, width):
        return jnp.dot(h, w_in_ref[:, off:off + width], preferred_element_type=f32)

    def proj_chunks(rows, off, width, act=lambda a: a):
        return [lambda c=c: act(proj(h_scr[rows, :], off + c, MXU_COLS)) for c in range(0, width, MXU_COLS)]

    def normed(x_ref, r):
        return _rms_norm(x_ref[r:r + BLOCK, :], vec_ref[VEC_NORM_G:VEC_NORM_G + 1, :]).astype(bf16)

    def store_kv(h_rows, dst):
        n = h_rows.shape[0]
        kv = proj(h_rows, OFF_K, 2 * KV_WIDTH)
        for scr, a in ((k_scr, kv[:, :KV_WIDTH]), (v_scr, kv[:, KV_WIDTH:])):
            swapped = pltpu.roll(a, HEAD_DIM, axis=1)
            scr[0, dst:dst + n, :] = jnp.where(low, a, swapped).astype(bf16)
            scr[1, dst:dst + n, :] = jnp.where(low, swapped, a).astype(bf16)

    def store_q(h_rows):
        q_scr[...] = (proj(h_rows, OFF_Q, ATTN_WIDTH) * (HEAD_DIM ** -0.5 * LOG2_E)).astype(bf16)

    @pl.when(step == 0)
    def _prepare_first_tile():
        vec_copies = [
            pltpu.make_async_copy(norm_g_hbm, vec_ref.at[pl.ds(VEC_NORM_G, 1), :], sems.at[3, 0]),
            pltpu.make_async_copy(final_g_hbm, vec_ref.at[pl.ds(VEC_FINAL_G, 1), :], sems.at[3, 1]),
            pltpu.make_async_copy(pool_scale_hbm, vec_ref.at[pl.ds(VEC_POOL_SCALE, 1), pl.ds(0, POOL_WIDTH)],
                                  sems.at[3, 2]),
        ]
        for cp in vec_copies:
            cp.start()
        row = lax.broadcasted_iota(jnp.int32, (GROUP * BLOCK, 1), 0)
        for hk in range(N_KV_HEADS):
            col = jnp.zeros((GROUP * BLOCK, 1), f32)
            for g in range(GROUP):
                col = jnp.where((row >= g * BLOCK) & (row < (g + 1) * BLOCK), sink_smem[hk * GROUP + g] * LOG2_E, col)
            sink_ref[hk] = col
        for cp in vec_copies:
            cp.wait()
        _cast_weights([
            ([(w_in_hbm, w_in_ref)], stage_in, sems.at[0]),
            ([(w_out_hbm, w_out_ref), (w_a_hbm, w_a_ref), (w_b_hbm, w_b_ref)], stage_sq, sems.at[1]),
            ([(pool_w_hbm, pool_w_ref)], stage_pool, sems.at[2]),
        ])
        zeros = jnp.zeros((HALO, D_MODEL), bf16)
        h_new = jnp.concatenate([normed(x_cur_ref, r) for r in range(0, TILE, BLOCK)]
                                + [normed(x_nxt_ref, 0)], axis=0)
        h_scr[0:HALO, :] = zeros
        h_scr[HALO:ROWS, :] = h_new
        store_kv(zeros, 0)
        store_kv(h_new, HALO)
        store_q(h_new[0:TILE])

    zero = jnp.zeros((), bf16)
    key_col = lax.broadcasted_iota(jnp.int32, (1, 3 * BLOCK), 1)

    def scores(hk, j):
        rows = slice(j * BLOCK, (j + 1) * BLOCK)
        pieces = []
        for pair in range(GROUP // 2):
            col = (hk * GROUP // 2 + pair) * 2 * HEAD_DIM
            qp = q_scr[rows, col:col + 2 * HEAD_DIM]
            pieces.append(jnp.where(low, qp, zero))
            pieces.append(jnp.where(low, zero, qp))
        lhs = jnp.concatenate(pieces, axis=0)
        kwin = k_scr[hk, j * BLOCK:j * BLOCK + 3 * BLOCK, :]
        s = lax.dot_general(lhs, kwin, (((1,), (1,)), ((), ())), preferred_element_type=f32)
        s = s + bias_ref[hk]
        if j == 0:
            s = s + jnp.where(is_first & (key_col < BLOCK), NEG_INF, 0.0)
        elif j == n_blk - 1:
            s = s + jnp.where(is_last & (key_col >= 2 * BLOCK), NEG_INF, 0.0)
        return s

    def softmax(s, hk):
        sink_col = sink_ref[hk]
        m = jnp.maximum(jnp.max(s, axis=-1, keepdims=True), sink_col)
        p = jnp.exp2(s - m)
        denom = jnp.sum(p, axis=-1, keepdims=True) + jnp.exp2(sink_col - m)
        return p.astype(bf16), denom

    def weighted_values(p, denom, hk, j):
        rows = slice(j * BLOCK, (j + 1) * BLOCK)
        vwin = v_scr[hk, j * BLOCK:j * BLOCK + 3 * BLOCK, :]
        o = jnp.dot(p, vwin, preferred_element_type=f32) / denom
        for pair in range(GROUP // 2):
            even = o[(2 * pair) * BLOCK:(2 * pair + 1) * BLOCK]
            odd = o[(2 * pair + 1) * BLOCK:(2 * pair + 2) * BLOCK]
            col = (hk * GROUP // 2 + pair) * 2 * HEAD_DIM
            attn_scr[rows, col:col + 2 * HEAD_DIM] = jnp.where(low, even, odd)

    def pooled_group(u, gi):
        w = POOL_WINDOWS[gi]
        r0, n = POOL_LEAD, TILE
        if w == 2:
            wsum = u[r0 - 1:r0 - 1 + n] + u[r0:r0 + n]
        else:
            c = u[0:n + 32] + u[1:n + 33]
            if w == 4:
                wsum = c[r0 - 2:r0 - 2 + n] + c[r0:r0 + n]
            else:
                e = c[0:n + 24] + c[2:n + 26]
                if w == 8:
                    wsum = e[r0 - 4:r0 - 4 + n] + e[r0:r0 + n]
                else:
                    f = e[0:n + 16] + e[4:n + 20]
                    wsum = f[r0 - 8:r0 - 8 + n] + f[r0:r0 + n]
        edge = POOL_EDGE_ROWS
        head = wsum[:edge] * cnt_ref[0, jnp.where(is_first, 1, 0), gi]
        body = wsum[edge:n - edge] * (1.0 / w)
        tail = wsum[n - edge:] * cnt_ref[1, jnp.where(is_last, 1, 0), gi]
        pooled = jnp.concatenate([head, body, tail], axis=0) - u[r0:r0 + n]
        pool_w = pool_w_ref[gi * POOL_GROUP:(gi + 1) * POOL_GROUP, :]
        return jnp.dot(pooled.astype(bf16), pool_w, preferred_element_type=f32)

    centre = slice(HALO, HALO + TILE)
    pool_rows = slice(HALO - POOL_LEAD, HALO + TILE + POOL_TAIL)
    work = [(hk, j) for j in range(n_blk) for hk in range(N_KV_HEADS)]
    fillers = (proj_chunks(centre, OFF_GA, D_MODEL, _sigmoid)
               + proj_chunks(centre, OFF_GB, D_MODEL, _sigmoid)
               + proj_chunks(centre, OFF_ZA, ATTN_WIDTH)
               + proj_chunks(pool_rows, OFF_U, POOL_WIDTH)
               + proj_chunks(centre, OFF_ZB, POOL_WIDTH))
    filled = []
    s_list = [scores(*work[n]) for n in range(SCORE_LEAD)]
    p_prev = None
    for n, (hk, j) in enumerate(work):
        filled += [f() for f in fillers[FILL_PER_STEP * n:FILL_PER_STEP * (n + 1)]]
        if n + SCORE_LEAD < len(work):
            s_list.append(scores(*work[n + SCORE_LEAD]))
        p_cur = softmax(s_list[n], hk)
        if p_prev is not None:
            weighted_values(*p_prev, *work[n - 1])
        p_prev = p_cur
    filled += [f() for f in fillers[FILL_PER_STEP * len(work):]]
    weighted_values(*p_prev, *work[-1])

    def take(width):
        chunks = [filled.pop(0) for _ in range(width // MXU_COLS)]
        return jnp.concatenate(chunks, axis=1)

    gate_a = take(D_MODEL)
    gate_b = take(D_MODEL)
    z_a = take(ATTN_WIDTH)
    u_all = take(POOL_WIDTH)
    z_b = take(POOL_WIDTH)
    u_all = jnp.concatenate([jnp.where(is_first, 0.0, u_all[:POOL_LEAD]),
                             u_all[POOL_LEAD:POOL_LEAD + TILE],
                             jnp.where(is_last, 0.0, u_all[POOL_LEAD + TILE:])], axis=0)

    gated_a = (attn_scr[...] * _silu(z_a)).astype(bf16)
    y_a = jnp.dot(gated_a, w_a_ref[...], preferred_element_type=f32)
    mixed = [pooled_group(u_all[:, gi * POOL_GROUP:(gi + 1) * POOL_GROUP], gi)
             for gi in range(len(POOL_WINDOWS))]
    mixed = jnp.concatenate(mixed, axis=1) * vec_ref[VEC_POOL_SCALE:VEC_POOL_SCALE + 1, 0:POOL_WIDTH]
    gated_b = (mixed * _silu(z_b)).astype(bf16)
    y_b = jnp.dot(gated_b, w_b_ref[...], preferred_element_type=f32)
    t_a = gate_a * y_a

    final_g = vec_ref[VEC_FINAL_G:VEC_FINAL_G + 1, :]
    for r in range(0, TILE, TAIL_ROWS):
        rows = slice(r, r + TAIL_ROWS)
        merged = (t_a[rows] + gate_b[rows] * y_b[rows]).astype(bf16)
        y = x_cur_ref[rows, :] + jnp.dot(merged, w_out_ref[...], preferred_element_type=f32)
        out_ref[rows, :] = _rms_norm(y, final_g)

    h_new = jnp.concatenate([normed(x_nxt_ref, r) for r in range(HALO, TILE, BLOCK)]
                            + [normed(x_after_ref, 0)], axis=0)
    h_keep = h_scr[TILE:ROWS, :]
    k_keep, v_keep = k_scr[:, TILE:ROWS, :], v_scr[:, TILE:ROWS, :]
    h_scr[0:2 * HALO, :] = h_keep
    h_scr[2 * HALO:ROWS, :] = h_new
    k_scr[:, 0:2 * HALO, :] = k_keep
    v_scr[:, 0:2 * HALO, :] = v_keep
    for r in range(0, TILE, 2 * HALO):
        store_kv(h_new[r:r + 2 * HALO], 2 * HALO + r)
    store_q(jnp.concatenate([h_keep[HALO:], h_new[:TILE - HALO]], axis=0))


def kernel(x, norm_g, w_in, attn_sink, pool_w, pool_scale, w_branch_a, w_branch_b, w_out, final_norm_g):
    batch, seq_len, d_model = x.shape
    depth = norm_g.shape[0]
    assert depth == 1, "the final RMSNorm is fused into the single layer's kernel"
    assert d_model == D_MODEL and w_in.shape[-1] == IN_WIDTH
    assert seq_len % TILE == 0 and seq_len // BLOCK >= 2 and seq_len >= 2 * POOL_EDGE_ROWS
    assert POOL_LEAD >= max(POOL_WINDOWS) // 2 and POOL_EDGE_ROWS >= max(POOL_WINDOWS) // 2
    assert TILE // BLOCK >= 2 and TILE // TAIL_ROWS >= 2
    bf16 = jnp.bfloat16
    bias = jnp.asarray(_attention_bias())
    inv_count = jnp.asarray(_pool_inv_count(seq_len))
    tiles_per_seq = seq_len // TILE
    n_steps = batch * tiles_per_seq
    blk_per_tile = TILE // BLOCK
    last_blk = n_steps * blk_per_tile - 1
    pool_rows = len(POOL_WINDOWS) * POOL_GROUP

    def resident(shape):
        return pl.BlockSpec(shape, lambda s: (0,) * len(shape), pipeline_mode=pl.Buffered(1))

    in_hbm = pl.BlockSpec(memory_space=pl.ANY)
    call = pl.pallas_call(
        functools.partial(_block_kernel, tiles_per_seq=tiles_per_seq),
        grid=(n_steps,),
        in_specs=[
            pl.BlockSpec((TILE, D_MODEL), lambda s: (s, 0)),
            pl.BlockSpec((TILE, D_MODEL), lambda s: (jnp.minimum(s + 1, n_steps - 1), 0)),
            pl.BlockSpec((BLOCK, D_MODEL), lambda s: (jnp.minimum((s + 2) * blk_per_tile, last_blk), 0)),
            in_hbm,
            in_hbm,
            pl.BlockSpec(memory_space=pltpu.SMEM),
            resident((N_KV_HEADS, GROUP * BLOCK, 3 * BLOCK)),
            resident(inv_count.shape),
            in_hbm,
            in_hbm,
            in_hbm,
            in_hbm,
            in_hbm,
            in_hbm,
        ],
        out_specs=pl.BlockSpec((TILE, D_MODEL), lambda s: (s, 0)),
        out_shape=jax.ShapeDtypeStruct((batch * seq_len, d_model), x.dtype),
        scratch_shapes=[
            pltpu.VMEM((D_MODEL, IN_WIDTH), bf16),
            pltpu.VMEM((pool_rows, POOL_GROUP), bf16),
            pltpu.VMEM((ATTN_WIDTH, D_MODEL), bf16),
            pltpu.VMEM((POOL_WIDTH, D_MODEL), bf16),
            pltpu.VMEM((D_MODEL, D_MODEL), bf16),
            pltpu.VMEM((W_SLOTS, W_IN_CHUNK, IN_WIDTH), jnp.float32),
            pltpu.VMEM((W_SLOTS, W_SQ_CHUNK, D_MODEL), jnp.float32),
            pltpu.VMEM((W_SLOTS, W_SQ_CHUNK, POOL_GROUP), jnp.float32),
            pltpu.SemaphoreType.DMA((4, W_SLOTS)),
            pltpu.VMEM((VEC_ROWS, D_MODEL), jnp.float32),
            pltpu.VMEM((N_KV_HEADS, GROUP * BLOCK, 1), jnp.float32),
            pltpu.VMEM((ROWS, D_MODEL), bf16),
            pltpu.VMEM((N_KV_HEADS, ROWS, 2 * HEAD_DIM), bf16),
            pltpu.VMEM((N_KV_HEADS, ROWS, 2 * HEAD_DIM), bf16),
            pltpu.VMEM((TILE, ATTN_WIDTH), bf16),
            pltpu.VMEM((TILE, ATTN_WIDTH), jnp.float32),
        ],
        compiler_params=pltpu.CompilerParams(
            dimension_semantics=("arbitrary",),
            vmem_limit_bytes=VMEM_LIMIT_BYTES),
        name="hybrid_block",
    )

    x2d = x.reshape(batch * seq_len, d_model)
    y2d = call(x2d, x2d, x2d,
               norm_g, w_in[0], attn_sink[0].astype(jnp.float32), bias, inv_count,
               pool_w[0].reshape(pool_rows, POOL_GROUP), pool_scale,
               w_branch_a[0], w_branch_b[0], w_out[0], final_norm_g.reshape(1, D_MODEL))
    return y2d.reshape(batch, seq_len, d_model)
```

```python
import functools

import numpy as np
import jax
import jax.numpy as jnp
from jax import lax
from jax.experimental import pallas as pl
from jax.experimental.pallas import tpu as pltpu

D_MODEL = 1024
N_Q_HEADS = 8
N_KV_HEADS = 2
GROUP = N_Q_HEADS // N_KV_HEADS
HEAD_DIM = 64
ATTN_WIDTH = N_Q_HEADS * HEAD_DIM
KV_WIDTH = N_KV_HEADS * HEAD_DIM
WINDOW = 128
BLOCK = 128
POOL_WIDTH = 512
POOL_WINDOWS = (2, 4, 8, 16)
POOL_GROUP = 128
EPS = 1e-6
NEG_INF = -1e30
LOG2_E = 1.4426950408889634

OFF_Q = 0
OFF_K = OFF_Q + ATTN_WIDTH
OFF_V = OFF_K + KV_WIDTH
OFF_ZA = OFF_V + KV_WIDTH
OFF_U = OFF_ZA + ATTN_WIDTH
OFF_ZB = OFF_U + POOL_WIDTH
OFF_GA = OFF_ZB + POOL_WIDTH
OFF_GB = OFF_GA + D_MODEL
IN_WIDTH = OFF_GB + D_MODEL

TILE = 512
HALO = BLOCK
ROWS = TILE + 2 * HALO
MXU_COLS = 256
TAIL_ROWS = 128
SCORE_LEAD = 2
FILL_PER_STEP = 1
POOL_EDGE_ROWS = 16
POOL_LEAD = 16
POOL_TAIL = 32
W_IN_CHUNK = 64
W_SQ_CHUNK = 128
W_SLOTS = 3
VMEM_LIMIT_BYTES = 56 * 1024 * 1024
VEC_ROWS = 4
VEC_NORM_G, VEC_FINAL_G, VEC_POOL_SCALE = 0, 1, 2


def _attention_bias():
    r = np.arange(BLOCK)[:, None]
    c = np.arange(3 * BLOCK)[None, :]
    dist = np.abs(r - (c - BLOCK))
    in_window = dist <= WINDOW
    slopes = np.exp2(-8.0 * np.arange(1, N_Q_HEADS + 1, dtype=np.float64) / N_Q_HEADS)
    alibi = -slopes[:, None, None] * dist[None].astype(np.float64)
    out = np.where(in_window, alibi * LOG2_E, NEG_INF)
    return out.reshape(N_KV_HEADS, GROUP * BLOCK, 3 * BLOCK).astype(np.float32)


def _pool_inv_count(seq_len):
    out = np.zeros((2, 2, len(POOL_WINDOWS), POOL_EDGE_ROWS, 1), np.float64)
    for gi, w in enumerate(POOL_WINDOWS):
        half = w // 2
        out[:, 0, gi] = 1.0 / w
        for end, pos in enumerate([np.arange(POOL_EDGE_ROWS), seq_len - POOL_EDGE_ROWS + np.arange(POOL_EDGE_ROWS)]):
            count = np.minimum(pos + half, seq_len) - np.maximum(pos - half, 0)
            out[end, 1, gi, :, 0] = 1.0 / count
    return np.broadcast_to(out, out.shape[:-1] + (POOL_GROUP,)).astype(np.float32)


def _rms_norm(x, g):
    ms = jnp.mean(x * x, axis=-1, keepdims=True)
    return x * lax.rsqrt(ms + EPS) * g


def _sigmoid(x):
    return 0.5 * jnp.tanh(0.5 * x) + 0.5


def _silu(x):
    return x * _sigmoid(x)


def _cast_weights(streams):
    rings = []
    for pairs, stage_ref, sem_ref in streams:
        chunk = stage_ref.shape[1]
        jobs = [(src, dst, r) for src, dst in pairs for r in range(0, dst.shape[0], chunk)]
        rings.append((jobs, stage_ref, sem_ref))

    def copy(ring, c):
        jobs, stage_ref, sem_ref = ring
        src, _, r = jobs[c]
        slot = c % stage_ref.shape[0]
        return pltpu.make_async_copy(src.at[pl.ds(r, stage_ref.shape[1]), :], stage_ref.at[slot], sem_ref.at[slot])

    for ring in rings:
        for c in range(min(ring[1].shape[0], len(ring[0]))):
            copy(ring, c).start()
    for c in range(max(len(ring[0]) for ring in rings)):
        for ring in rings:
            jobs, stage_ref, _ = ring
            if c < len(jobs):
                n_slots, chunk = stage_ref.shape[0], stage_ref.shape[1]
                _, dst, r = jobs[c]
                copy(ring, c).wait()
                dst[r:r + chunk, :] = stage_ref[c % n_slots].astype(dst.dtype)
                if c + n_slots < len(jobs):
                    copy(ring, c + n_slots).start()


def _block_kernel(x_cur_ref, x_nxt_ref, x_after_ref, norm_g_hbm, w_in_hbm, sink_smem, bias_ref, cnt_ref,
                  pool_w_hbm, pool_scale_hbm, w_a_hbm, w_b_hbm, w_out_hbm, final_g_hbm,
                  out_ref, w_in_ref, pool_w_ref, w_a_ref, w_b_ref, w_out_ref, stage_in, stage_sq, stage_pool, sems,
                  vec_ref, sink_ref,
                  h_scr, k_scr, v_scr, q_scr, attn_scr, *, tiles_per_seq):
    f32, bf16 = jnp.float32, jnp.bfloat16
    step = pl.program_id(0)
    i = lax.rem(step, tiles_per_seq)
    n_blk = TILE // BLOCK
    is_first = i == 0
    is_last = i == tiles_per_seq - 1
    lane = lax.broadcasted_iota(jnp.int32, (1, 2 * HEAD_DIM), 1)
    low = lane < HEAD_DIM

    def proj(h, off, width):
        return jnp.dot(h, w_in_ref[:, off:off + width], preferred_element_type=f32)

    def proj_chunks(rows, off, width, act=lambda a: a):
        return [lambda c=c: act(proj(h_scr[rows, :], off + c, MXU_COLS)) for c in range(0, width, MXU_COLS)]

    def normed(x_ref, r):
        return _rms_norm(x_ref[r:r + BLOCK, :], vec_ref[VEC_NORM_G:VEC_NORM_G + 1, :]).astype(bf16)

    def store_kv(h_rows, dst):
        n = h_rows.shape[0]
        kv = proj(h_rows, OFF_K, 2 * KV_WIDTH)
        for scr, a in ((k_scr, kv[:, :KV_WIDTH]), (v_scr, kv[:, KV_WIDTH:])):
            swapped = pltpu.roll(a, HEAD_DIM, axis=1)
            scr[0, dst:dst + n, :] = jnp.where(low, a, swapped).astype(bf16)
            scr[1, dst:dst + n, :] = jnp.where(low, swapped, a).astype(bf16)

    def store_q(h_rows):
        q_scr[...] = (proj(h_rows, OFF_Q, ATTN_WIDTH) * (HEAD_DIM ** -0.5 * LOG2_E)).astype(bf16)

    @pl.when(step == 0)
    def _prepare_first_tile():
        vec_copies = [
            pltpu.make_async_copy(norm_g_hbm, vec_ref.at[pl.ds(VEC_NORM_G, 1), :], sems.at[3, 0]),
            pltpu.make_async_copy(final_g_hbm, vec_ref.at[pl.ds(VEC_FINAL_G, 1), :], sems.at[3, 1]),
            pltpu.make_async_copy(pool_scale_hbm, vec_ref.at[pl.ds(VEC_POOL_SCALE, 1), pl.ds(0, POOL_WIDTH)],
                                  sems.at[3, 2]),
        ]
        for cp in vec_copies:
            cp.start()
        row = lax.broadcasted_iota(jnp.int32, (GROUP * BLOCK, 1), 0)
        for hk in range(N_KV_HEADS):
            col = jnp.zeros((GROUP * BLOCK, 1), f32)
            for g in range(GROUP):
                col = jnp.where((row >= g * BLOCK) & (row < (g + 1) * BLOCK), sink_smem[hk * GROUP + g] * LOG2_E, col)
            sink_ref[hk] = col
        for cp in vec_copies:
            cp.wait()
        _cast_weights([
            ([(w_in_hbm, w_in_ref)], stage_in, sems.at[0]),
            ([(w_out_hbm, w_out_ref), (w_a_hbm, w_a_ref), (w_b_hbm, w_b_ref)], stage_sq, sems.at[1]),
            ([(pool_w_hbm, pool_w_ref)], stage_pool, sems.at[2]),
        ])
        zeros = jnp.zeros((HALO, D_MODEL), bf16)
        h_new = jnp.concatenate([normed(x_cur_ref, r) for r in range(0, TILE, BLOCK)]
                                + [normed(x_nxt_ref, 0)], axis=0)
        h_scr[0:HALO, :] = zeros
        h_scr[HALO:ROWS, :] = h_new
        store_kv(zeros, 0)
        store_kv(h_new, HALO)
        store_q(h_new[0:TILE])

    zero = jnp.zeros((), bf16)
    key_col = lax.broadcasted_iota(jnp.int32, (1, 3 * BLOCK), 1)

    def scores(hk, j):
        rows = slice(j * BLOCK, (j + 1) * BLOCK)
        pieces = []
        for pair in range(GROUP // 2):
            col = (hk * GROUP // 2 + pair) * 2 * HEAD_DIM
            qp = q_scr[rows, col:col + 2 * HEAD_DIM]
            pieces.append(jnp.where(low, qp, zero))
            pieces.append(jnp.where(low, zero, qp))
        lhs = jnp.concatenate(pieces, axis=0)
        kwin = k_scr[hk, j * BLOCK:j * BLOCK + 3 * BLOCK, :]
        s = lax.dot_general(lhs, kwin, (((1,), (1,)), ((), ())), preferred_element_type=f32)
        s = s + bias_ref[hk]
        if j == 0:
            s = s + jnp.where(is_first & (key_col < BLOCK), NEG_INF, 0.0)
        elif j == n_blk - 1:
            s = s + jnp.where(is_last & (key_col >= 2 * BLOCK), NEG_INF, 0.0)
        return s

    def softmax(s, hk):
        sink_col = sink_ref[hk]
        m = jnp.maximum(jnp.max(s, axis=-1, keepdims=True), sink_col)
        p = jnp.exp2(s - m)
        denom = jnp.sum(p, axis=-1, keepdims=True) + jnp.exp2(sink_col - m)
        return p.astype(bf16), denom

    def weighted_values(p, denom, hk, j):
        rows = slice(j * BLOCK, (j + 1) * BLOCK)
        vwin = v_scr[hk, j * BLOCK:j * BLOCK + 3 * BLOCK, :]
        o = jnp.dot(p, vwin, preferred_element_type=f32) / denom
        for pair in range(GROUP // 2):
            even = o[(2 * pair) * BLOCK:(2 * pair + 1) * BLOCK]
            odd = o[(2 * pair + 1) * BLOCK:(2 * pair + 2) * BLOCK]
            col = (hk * GROUP // 2 + pair) * 2 * HEAD_DIM
            attn_scr[rows, col:col + 2 * HEAD_DIM] = jnp.where(low, even, odd)

    def pooled_group(u, gi):
        w = POOL_WINDOWS[gi]
        r0, n = POOL_LEAD, TILE
        if w == 2:
            wsum = u[r0 - 1:r0 - 1 + n] + u[r0:r0 + n]
        else:
            c = u[0:n + 32] + u[1:n + 33]
            if w == 4:
                wsum = c[r0 - 2:r0 - 2 + n] + c[r0:r0 + n]
            else:
                e = c[0:n + 24] + c[2:n + 26]
                if w == 8:
                    wsum = e[r0 - 4:r0 - 4 + n] + e[r0:r0 + n]
                else:
                    f = e[0:n + 16] + e[4:n + 20]
                    wsum = f[r0 - 8:r0 - 8 + n] + f[r0:r0 + n]
        edge = POOL_EDGE_ROWS
        head = wsum[:edge] * cnt_ref[0, jnp.where(is_first, 1, 0), gi]
        body = wsum[edge:n - edge] * (1.0 / w)
        tail = wsum[n - edge:] * cnt_ref[1, jnp.where(is_last, 1, 0), gi]
        pooled = jnp.concatenate([head, body, tail], axis=0) - u[r0:r0 + n]
        pool_w = pool_w_ref[gi * POOL_GROUP:(gi + 1) * POOL_GROUP, :]
        return jnp.dot(pooled.astype(bf16), pool_w, preferred_element_type=f32)

    centre = slice(HALO, HALO + TILE)
    pool_rows = slice(HALO - POOL_LEAD, HALO + TILE + POOL_TAIL)
    work = [(hk, j) for hk in range(N_KV_HEADS) for j in range(n_blk)]
    fillers = (proj_chunks(centre, OFF_GB, D_MODEL, _sigmoid)
               + proj_chunks(centre, OFF_GA, D_MODEL, _sigmoid)
               + proj_chunks(centre, OFF_ZA, ATTN_WIDTH)
               + proj_chunks(pool_rows, OFF_U, POOL_WIDTH)
               + proj_chunks(centre, OFF_ZB, POOL_WIDTH))
    filled = []
    s_list = [scores(*work[n]) for n in range(SCORE_LEAD)]
    p_prev = None
    for n, (hk, j) in enumerate(work):
        filled += [f() for f in fillers[FILL_PER_STEP * n:FILL_PER_STEP * (n + 1)]]
        if n + SCORE_LEAD < len(work):
            s_list.append(scores(*work[n + SCORE_LEAD]))
        p_cur = softmax(s_list[n], hk)
        if p_prev is not None:
            weighted_values(*p_prev, *work[n - 1])
        p_prev = p_cur
    filled += [f() for f in fillers[FILL_PER_STEP * len(work):]]
    weighted_values(*p_prev, *work[-1])

    def take(width):
        chunks = [filled.pop(0) for _ in range(width // MXU_COLS)]
        return jnp.concatenate(chunks, axis=1)

    gate_b = take(D_MODEL)
    gate_a = take(D_MODEL)
    z_a = take(ATTN_WIDTH)
    u_all = take(POOL_WIDTH)
    z_b = take(POOL_WIDTH)
    u_all = jnp.concatenate([jnp.where(is_first, 0.0, u_all[:POOL_LEAD]),
                             u_all[POOL_LEAD:POOL_LEAD + TILE],
                             jnp.where(is_last, 0.0, u_all[POOL_LEAD + TILE:])], axis=0)

    gated_a = (attn_scr[...] * _silu(z_a)).astype(bf16)
    y_a = jnp.dot(gated_a, w_a_ref[...], preferred_element_type=f32)
    mixed = [pooled_group(u_all[:, gi * POOL_GROUP:(gi + 1) * POOL_GROUP], gi)
             for gi in range(len(POOL_WINDOWS))]
    mixed = jnp.concatenate(mixed, axis=1) * vec_ref[VEC_POOL_SCALE:VEC_POOL_SCALE + 1, 0:POOL_WIDTH]
    gated_b = (mixed * _silu(z_b)).astype(bf16)
    y_b = jnp.dot(gated_b, w_b_ref[...], preferred_element_type=f32)
    t_a = gate_a * y_a

    final_g = vec_ref[VEC_FINAL_G:VEC_FINAL_G + 1, :]
    for r in range(0, TILE, TAIL_ROWS):
        rows = slice(r, r + TAIL_ROWS)
        merged = (t_a[rows] + gate_b[rows] * y_b[rows]).astype(bf16)
        y = x_cur_ref[rows, :] + jnp.dot(merged, w_out_ref[...], preferred_element_type=f32)
        out_ref[rows, :] = _rms_norm(y, final_g)

    h_new = jnp.concatenate([normed(x_nxt_ref, r) for r in range(HALO, TILE, BLOCK)]
                            + [normed(x_after_ref, 0)], axis=0)
    h_keep = h_scr[TILE:ROWS, :]
    k_keep, v_keep = k_scr[:, TILE:ROWS, :], v_scr[:, TILE:ROWS, :]
    h_scr[0:2 * HALO, :] = h_keep
    h_scr[2 * HALO:ROWS, :] = h_new
    k_scr[:, 0:2 * HALO, :] = k_keep
    v_scr[:, 0:2 * HALO, :] = v_keep
    for r in range(0, TILE, 2 * HALO):
        store_kv(h_new[r:r + 2 * HALO], 2 * HALO + r)
    store_q(jnp.concatenate([h_keep[HALO:], h_new[:TILE - HALO]], axis=0))


def kernel(x, norm_g, w_in, attn_sink, pool_w, pool_scale, w_branch_a, w_branch_b, w_out, final_norm_g):
    batch, seq_len, d_model = x.shape
    depth = norm_g.shape[0]
    assert depth == 1, "the final RMSNorm is fused into the single layer's kernel"
    assert d_model == D_MODEL and w_in.shape[-1] == IN_WIDTH
    assert seq_len % TILE == 0 and seq_len // BLOCK >= 2 and seq_len >= 2 * POOL_EDGE_ROWS
    assert POOL_LEAD >= max(POOL_WINDOWS) // 2 and POOL_EDGE_ROWS >= max(POOL_WINDOWS) // 2
    assert TILE // BLOCK >= 2 and TILE // TAIL_ROWS >= 2
    bf16 = jnp.bfloat16
    bias = jnp.asarray(_attention_bias())
    inv_count = jnp.asarray(_pool_inv_count(seq_len))
    tiles_per_seq = seq_len // TILE
    n_steps = batch * tiles_per_seq
    blk_per_tile = TILE // BLOCK
    last_blk = n_steps * blk_per_tile - 1
    pool_rows = len(POOL_WINDOWS) * POOL_GROUP

    def resident(shape):
        return pl.BlockSpec(shape, lambda s: (0,) * len(shape), pipeline_mode=pl.Buffered(1))

    in_hbm = pl.BlockSpec(memory_space=pl.ANY)
    call = pl.pallas_call(
        functools.partial(_block_kernel, tiles_per_seq=tiles_per_seq),
        grid=(n_steps,),
        in_specs=[
            pl.BlockSpec((TILE, D_MODEL), lambda s: (s, 0)),
            pl.BlockSpec((TILE, D_MODEL), lambda s: (jnp.minimum(s + 1, n_steps - 1), 0)),
            pl.BlockSpec((BLOCK, D_MODEL), lambda s: (jnp.minimum((s + 2) * blk_per_tile, last_blk), 0)),
            in_hbm,
            in_hbm,
            pl.BlockSpec(memory_space=pltpu.SMEM),
            resident((N_KV_HEADS, GROUP * BLOCK, 3 * BLOCK)),
            resident(inv_count.shape),
            in_hbm,
            in_hbm,
            in_hbm,
            in_hbm,
            in_hbm,
            in_hbm,
        ],
        out_specs=pl.BlockSpec((TILE, D_MODEL), lambda s: (s, 0)),
        out_shape=jax.ShapeDtypeStruct((batch * seq_len, d_model), x.dtype),
        scratch_shapes=[
            pltpu.VMEM((D_MODEL, IN_WIDTH), bf16),
            pltpu.VMEM((pool_rows, POOL_GROUP), bf16),
            pltpu.VMEM((ATTN_WIDTH, D_MODEL), bf16),
            pltpu.VMEM((POOL_WIDTH, D_MODEL), bf16),
            pltpu.VMEM((D_MODEL, D_MODEL), bf16),
            pltpu.VMEM((W_SLOTS, W_IN_CHUNK, IN_WIDTH), jnp.float32),
            pltpu.VMEM((W_SLOTS, W_SQ_CHUNK, D_MODEL), jnp.float32),
            pltpu.VMEM((W_SLOTS, W_SQ_CHUNK, POOL_GROUP), jnp.float32),
            pltpu.SemaphoreType.DMA((4, W_SLOTS)),
            pltpu.VMEM((VEC_ROWS, D_MODEL), jnp.float32),
            pltpu.VMEM((N_KV_HEADS, GROUP * BLOCK, 1), jnp.float32),
            pltpu.VMEM((ROWS, D_MODEL), bf16),
            pltpu.VMEM((N_KV_HEADS, ROWS, 2 * HEAD_DIM), bf16),
            pltpu.VMEM((N_KV_HEADS, ROWS, 2 * HEAD_DIM), bf16),
            pltpu.VMEM((TILE, ATTN_WIDTH), bf16),
            pltpu.VMEM((TILE, ATTN_WIDTH), jnp.float32),
        ],
        compiler_params=pltpu.CompilerParams(
            dimension_semantics=("arbitrary",),
            vmem_limit_bytes=VMEM_LIMIT_BYTES),
        name="hybrid_block",
    )

    x2d = x.reshape(batch * seq_len, d_model)
    y2d = call(x2d, x2d, x2d,
               norm_g, w_in[0], attn_sink[0].astype(jnp.float32), bias, inv_count,
               pool_w[0].reshape(pool_rows, POOL_GROUP), pool_scale,
               w_branch_a[0], w_branch_b[0], w_out[0], final_norm_g.reshape(1, D_MODEL))
    return y2d.reshape(batch, seq_len, d_model)
```

```python
import functools

import numpy as np
import jax
import jax.numpy as jnp
from jax import lax
from jax.experimental import pallas as pl
from jax.experimental.pallas import tpu as pltpu

D_MODEL = 1024
N_Q_HEADS = 8
N_KV_HEADS = 2
GROUP = N_Q_HEADS // N_KV_HEADS
HEAD_DIM = 64
ATTN_WIDTH = N_Q_HEADS * HEAD_DIM
KV_WIDTH = N_KV_HEADS * HEAD_DIM
WINDOW = 128
BLOCK = 128
POOL_WIDTH = 512
POOL_WINDOWS = (2, 4, 8, 16)
POOL_GROUP = 128
EPS = 1e-6
NEG_INF = -1e30
LOG2_E = 1.4426950408889634

OFF_Q = 0
OFF_K = OFF_Q + ATTN_WIDTH
OFF_V = OFF_K + KV_WIDTH
OFF_ZA = OFF_V + KV_WIDTH
OFF_U = OFF_ZA + ATTN_WIDTH
OFF_ZB = OFF_U + POOL_WIDTH
OFF_GA = OFF_ZB + POOL_WIDTH
OFF_GB = OFF_GA + D_MODEL
IN_WIDTH = OFF_GB + D_MODEL

TILE = 512
HALO = BLOCK
ROWS = TILE + 2 * HALO
MXU_COLS = 256
TAIL_ROWS = 256
SCORE_LEAD = 2
FILL_PER_STEP = 1
POOL_EDGE_ROWS = 16
POOL_LEAD = 16
POOL_TAIL = 32
W_IN_CHUNK = 64
W_SQ_CHUNK = 128
W_SLOTS = 3
VMEM_LIMIT_BYTES = 56 * 1024 * 1024
VEC_ROWS = 4
VEC_NORM_G, VEC_FINAL_G, VEC_POOL_SCALE = 0, 1, 2


def _attention_bias():
    r = np.arange(BLOCK)[:, None]
    c = np.arange(3 * BLOCK)[None, :]
    dist = np.abs(r - (c - BLOCK))
    in_window = dist <= WINDOW
    slopes = np.exp2(-8.0 * np.arange(1, N_Q_HEADS + 1, dtype=np.float64) / N_Q_HEADS)
    alibi = -slopes[:, None, None] * dist[None].astype(np.float64)
    out = np.where(in_window, alibi * LOG2_E, NEG_INF)
    return out.reshape(N_KV_HEADS, GROUP * BLOCK, 3 * BLOCK).astype(np.float32)


def _pool_inv_count(seq_len):
    out = np.zeros((2, 2, len(POOL_WINDOWS), POOL_EDGE_ROWS, 1), np.float64)
    for gi, w in enumerate(POOL_WINDOWS):
        half = w // 2
        out[:, 0, gi] = 1.0 / w
        for end, pos in enumerate([np.arange(POOL_EDGE_ROWS), seq_len - POOL_EDGE_ROWS + np.arange(POOL_EDGE_ROWS)]):
            count = np.minimum(pos + half, seq_len) - np.maximum(pos - half, 0)
            out[end, 1, gi, :, 0] = 1.0 / count
    return np.broadcast_to(out, out.shape[:-1] + (POOL_GROUP,)).astype(np.float32)


def _rms_norm(x, g):
    ms = jnp.mean(x * x, axis=-1, keepdims=True)
    return x * lax.rsqrt(ms + EPS) * g


def _sigmoid(x):
    return 0.5 * jnp.tanh(0.5 * x) + 0.5


def _silu(x):
    return x * _sigmoid(x)


def _cast_weights(streams):
    rings = []
    for pairs, stage_ref, sem_ref in streams:
        chunk = stage_ref.shape[1]
        jobs = [(src, dst, r) for src, dst in pairs for r in range(0, dst.shape[0], chunk)]
        rings.append((jobs, stage_ref, sem_ref))

    def copy(ring, c):
        jobs, stage_ref, sem_ref = ring
        src, _, r = jobs[c]
        slot = c % stage_ref.shape[0]
        return pltpu.make_async_copy(src.at[pl.ds(r, stage_ref.shape[1]), :], stage_ref.at[slot], sem_ref.at[slot])

    for ring in rings:
        for c in range(min(ring[1].shape[0], len(ring[0]))):
            copy(ring, c).start()
    for c in range(max(len(ring[0]) for ring in rings)):
        for ring in rings:
            jobs, stage_ref, _ = ring
            if c < len(jobs):
                n_slots, chunk = stage_ref.shape[0], stage_ref.shape[1]
                _, dst, r = jobs[c]
                copy(ring, c).wait()
                dst[r:r + chunk, :] = stage_ref[c % n_slots].astype(dst.dtype)
                if c + n_slots < len(jobs):
                    copy(ring, c + n_slots).start()


def _block_kernel(x_cur_ref, x_nxt_ref, x_after_ref, norm_g_hbm, w_in_hbm, sink_smem, bias_ref, cnt_ref,
                  pool_w_hbm, pool_scale_hbm, w_a_hbm, w_b_hbm, w_out_hbm, final_g_hbm,
                  out_ref, w_in_ref, pool_w_ref, w_a_ref, w_b_ref, w_out_ref, stage_in, stage_sq, stage_pool, sems,
                  vec_ref, sink_ref,
                  h_scr, k_scr, v_scr, q_scr, attn_scr, *, tiles_per_seq):
    f32, bf16 = jnp.float32, jnp.bfloat16
    step = pl.program_id(0)
    i = lax.rem(step, tiles_per_seq)
    n_blk = TILE // BLOCK
    is_first = i == 0
    is_last = i == tiles_per_seq - 1
    lane = lax.broadcasted_iota(jnp.int32, (1, 2 * HEAD_DIM), 1)
    low = lane < HEAD_DIM

    def proj(h, off, width):
        return jnp.dot(h, w_in_ref[:, off:off + width], preferred_element_type=f32)

    def proj_chunks(rows, off, width, act=lambda a: a):
        return [lambda c=c: act(proj(h_scr[rows, :], off + c, MXU_COLS)) for c in range(0, width, MXU_COLS)]

    def normed(x_ref, r):
        return _rms_norm(x_ref[r:r + BLOCK, :], vec_ref[VEC_NORM_G:VEC_NORM_G + 1, :]).astype(bf16)

    def store_kv(h_rows, dst):
        n = h_rows.shape[0]
        kv = proj(h_rows, OFF_K, 2 * KV_WIDTH)
        for scr, a in ((k_scr, kv[:, :KV_WIDTH]), (v_scr, kv[:, KV_WIDTH:])):
            swapped = pltpu.roll(a, HEAD_DIM, axis=1)
            scr[0, dst:dst + n, :] = jnp.where(low, a, swapped).astype(bf16)
            scr[1, dst:dst + n, :] = jnp.where(low, swapped, a).astype(bf16)

    def store_q(h_rows):
        q_scr[...] = (proj(h_rows, OFF_Q, ATTN_WIDTH) * (HEAD_DIM ** -0.5 * LOG2_E)).astype(bf16)

    @pl.when(step == 0)
    def _prepare_first_tile():
        vec_copies = [
            pltpu.make_async_copy(norm_g_hbm, vec_ref.at[pl.ds(VEC_NORM_G, 1), :], sems.at[3, 0]),
            pltpu.make_async_copy(final_g_hbm, vec_ref.at[pl.ds(VEC_FINAL_G, 1), :], sems.at[3, 1]),
            pltpu.make_async_copy(pool_scale_hbm, vec_ref.at[pl.ds(VEC_POOL_SCALE, 1), pl.ds(0, POOL_WIDTH)],
                                  sems.at[3, 2]),
        ]
        for cp in vec_copies:
            cp.start()
        row = lax.broadcasted_iota(jnp.int32, (GROUP * BLOCK, 1), 0)
        for hk in range(N_KV_HEADS):
            col = jnp.zeros((GROUP * BLOCK, 1), f32)
            for g in range(GROUP):
                col = jnp.where((row >= g * BLOCK) & (row < (g + 1) * BLOCK), sink_smem[hk * GROUP + g] * LOG2_E, col)
            sink_ref[hk] = col
        for cp in vec_copies:
            cp.wait()
        _cast_weights([
            ([(w_in_hbm, w_in_ref)], stage_in, sems.at[0]),
            ([(w_out_hbm, w_out_ref), (w_a_hbm, w_a_ref), (w_b_hbm, w_b_ref)], stage_sq, sems.at[1]),
            ([(pool_w_hbm, pool_w_ref)], stage_pool, sems.at[2]),
        ])
        zeros = jnp.zeros((HALO, D_MODEL), bf16)
        h_new = jnp.concatenate([normed(x_cur_ref, r) for r in range(0, TILE, BLOCK)]
                                + [normed(x_nxt_ref, 0)], axis=0)
        h_scr[0:HALO, :] = zeros
        h_scr[HALO:ROWS, :] = h_new
        store_kv(zeros, 0)
        store_kv(h_new, HALO)
        store_q(h_new[0:TILE])

    zero = jnp.zeros((), bf16)
    key_col = lax.broadcasted_iota(jnp.int32, (1, 3 * BLOCK), 1)

    def scores(hk, j):
        rows = slice(j * BLOCK, (j + 1) * BLOCK)
        pieces = []
        for pair in range(GROUP // 2):
            col = (hk * GROUP // 2 + pair) * 2 * HEAD_DIM
            qp = q_scr[rows, col:col + 2 * HEAD_DIM]
            pieces.append(jnp.where(low, qp, zero))
            pieces.append(jnp.where(low, zero, qp))
        lhs = jnp.concatenate(pieces, axis=0)
        kwin = k_scr[hk, j * BLOCK:j * BLOCK + 3 * BLOCK, :]
        s = lax.dot_general(lhs, kwin, (((1,), (1,)), ((), ())), preferred_element_type=f32)
        s = s + bias_ref[hk]
        if j == 0:
            s = s + jnp.where(is_first & (key_col < BLOCK), NEG_INF, 0.0)
        elif j == n_blk - 1:
            s = s + jnp.where(is_last & (key_col >= 2 * BLOCK), NEG_INF, 0.0)
        return s

    def softmax(s, hk):
        sink_col = sink_ref[hk]
        m = jnp.maximum(jnp.max(s, axis=-1, keepdims=True), sink_col)
        p = jnp.exp2(s - m)
        denom = jnp.sum(p, axis=-1, keepdims=True) + jnp.exp2(sink_col - m)
        return p.astype(bf16), denom

    def weighted_values(p, denom, hk, j):
        rows = slice(j * BLOCK, (j + 1) * BLOCK)
        vwin = v_scr[hk, j * BLOCK:j * BLOCK + 3 * BLOCK, :]
        o = jnp.dot(p, vwin, preferred_element_type=f32) / denom
        for pair in range(GROUP // 2):
            even = o[(2 * pair) * BLOCK:(2 * pair + 1) * BLOCK]
            odd = o[(2 * pair + 1) * BLOCK:(2 * pair + 2) * BLOCK]
            col = (hk * GROUP // 2 + pair) * 2 * HEAD_DIM
            attn_scr[rows, col:col + 2 * HEAD_DIM] = jnp.where(low, even, odd)

    def pooled_group(u, gi):
        w = POOL_WINDOWS[gi]
        r0, n = POOL_LEAD, TILE
        if w == 2:
            wsum = u[r0 - 1:r0 - 1 + n] + u[r0:r0 + n]
        else:
            c = u[0:n + 32] + u[1:n + 33]
            if w == 4:
                wsum = c[r0 - 2:r0 - 2 + n] + c[r0:r0 + n]
            else:
                e = c[0:n + 24] + c[2:n + 26]
                if w == 8:
                    wsum = e[r0 - 4:r0 - 4 + n] + e[r0:r0 + n]
                else:
                    f = e[0:n + 16] + e[4:n + 20]
                    wsum = f[r0 - 8:r0 - 8 + n] + f[r0:r0 + n]
        edge = POOL_EDGE_ROWS
        head = wsum[:edge] * cnt_ref[0, jnp.where(is_first, 1, 0), gi]
        body = wsum[edge:n - edge] * (1.0 / w)
        tail = wsum[n - edge:] * cnt_ref[1, jnp.where(is_last, 1, 0), gi]
        pooled = jnp.concatenate([head, body, tail], axis=0) - u[r0:r0 + n]
        pool_w = pool_w_ref[gi * POOL_GROUP:(gi + 1) * POOL_GROUP, :]
        return jnp.dot(pooled.astype(bf16), pool_w, preferred_element_type=f32)

    centre = slice(HALO, HALO + TILE)
    pool_rows = slice(HALO - POOL_LEAD, HALO + TILE + POOL_TAIL)
    work = [(hk, j) for hk in range(N_KV_HEADS) for j in range(n_blk)]
    fillers = (proj_chunks(centre, OFF_GA, D_MODEL, _sigmoid)
               + proj_chunks(centre, OFF_GB, D_MODEL, _sigmoid)
               + proj_chunks(centre, OFF_ZA, ATTN_WIDTH)
               + proj_chunks(pool_rows, OFF_U, POOL_WIDTH)
               + proj_chunks(centre, OFF_ZB, POOL_WIDTH))
    filled = []
    s_list = [scores(*work[n]) for n in range(SCORE_LEAD)]
    p_prev = None
    for n, (hk, j) in enumerate(work):
        filled += [f() for f in fillers[FILL_PER_STEP * n:FILL_PER_STEP * (n + 1)]]
        if n + SCORE_LEAD < len(work):
            s_list.append(scores(*work[n + SCORE_LEAD]))
        p_cur = softmax(s_list[n], hk)
        if p_prev is not None:
            weighted_values(*p_prev, *work[n - 1])
        p_prev = p_cur
    filled += [f() for f in fillers[FILL_PER_STEP * len(work):]]
    weighted_values(*p_prev, *work[-1])

    def take(width):
        chunks = [filled.pop(0) for _ in range(width // MXU_COLS)]
        return jnp.concatenate(chunks, axis=1)

    gate_a = take(D_MODEL)
    gate_b = take(D_MODEL)
    z_a = take(ATTN_WIDTH)
    u_all = take(POOL_WIDTH)
    z_b = take(POOL_WIDTH)
    u_all = jnp.concatenate([jnp.where(is_first, 0.0, u_all[:POOL_LEAD]),
                             u_all[POOL_LEAD:POOL_LEAD + TILE],
                             jnp.where(is_last, 0.0, u_all[POOL_LEAD + TILE:])], axis=0)

    gated_a = (attn_scr[...] * _silu(z_a)).astype(bf16)
    y_a = jnp.dot(gated_a, w_a_ref[...], preferred_element_type=f32)
    mixed = [pooled_group(u_all[:, gi * POOL_GROUP:(gi + 1) * POOL_GROUP], gi)
             for gi in range(len(POOL_WINDOWS))]
    mixed = jnp.concatenate(mixed, axis=1) * vec_ref[VEC_POOL_SCALE:VEC_POOL_SCALE + 1, 0:POOL_WIDTH]
    gated_b = (mixed * _silu(z_b)).astype(bf16)
    y_b = jnp.dot(gated_b, w_b_ref[...], preferred_element_type=f32)
    t_a = gate_a * y_a

    final_g = vec_ref[VEC_FINAL_G:VEC_FINAL_G + 1, :]
    for r in range(0, TILE, TAIL_ROWS):
        rows = slice(r, r + TAIL_ROWS)
        merged = (t_a[rows] + gate_b[rows] * y_b[rows]).astype(bf16)
        y = x_cur_ref[rows, :] + jnp.dot(merged, w_out_ref[...], preferred_element_type=f32)
        out_ref[rows, :] = _rms_norm(y, final_g)

    h_new = jnp.concatenate([normed(x_nxt_ref, r) for r in range(HALO, TILE, BLOCK)]
                            + [normed(x_after_ref, 0)], axis=0)
    h_keep = h_scr[TILE:ROWS, :]
    k_keep, v_keep = k_scr[:, TILE:ROWS, :], v_scr[:, TILE:ROWS, :]
    h_scr[0:2 * HALO, :] = h_keep
    h_scr[2 * HALO:ROWS, :] = h_new
    k_scr[:, 0:2 * HALO, :] = k_keep
    v_scr[:, 0:2 * HALO, :] = v_keep
    for r in range(0, TILE, 2 * HALO):
        store_kv(h_new[r:r + 2 * HALO], 2 * HALO + r)
    store_q(jnp.concatenate([h_keep[HALO:], h_new[:TILE - HALO]], axis=0))


def kernel(x, norm_g, w_in, attn_sink, pool_w, pool_scale, w_branch_a, w_branch_b, w_out, final_norm_g):
    batch, seq_len, d_model = x.shape
    depth = norm_g.shape[0]
    assert depth == 1, "the final RMSNorm is fused into the single layer's kernel"
    assert d_model == D_MODEL and w_in.shape[-1] == IN_WIDTH
    assert seq_len % TILE == 0 and seq_len // BLOCK >= 2 and seq_len >= 2 * POOL_EDGE_ROWS
    assert POOL_LEAD >= max(POOL_WINDOWS) // 2 and POOL_EDGE_ROWS >= max(POOL_WINDOWS) // 2
    assert TILE // BLOCK >= 2 and TILE // TAIL_ROWS >= 2
    bf16 = jnp.bfloat16
    bias = jnp.asarray(_attention_bias())
    inv_count = jnp.asarray(_pool_inv_count(seq_len))
    tiles_per_seq = seq_len // TILE
    n_steps = batch * tiles_per_seq
    blk_per_tile = TILE // BLOCK
    last_blk = n_steps * blk_per_tile - 1
    pool_rows = len(POOL_WINDOWS) * POOL_GROUP

    def resident(shape):
        return pl.BlockSpec(shape, lambda s: (0,) * len(shape), pipeline_mode=pl.Buffered(1))

    in_hbm = pl.BlockSpec(memory_space=pl.ANY)
    call = pl.pallas_call(
        functools.partial(_block_kernel, tiles_per_seq=tiles_per_seq),
        grid=(n_steps,),
        in_specs=[
            pl.BlockSpec((TILE, D_MODEL), lambda s: (s, 0)),
            pl.BlockSpec((TILE, D_MODEL), lambda s: (jnp.minimum(s + 1, n_steps - 1), 0)),
            pl.BlockSpec((BLOCK, D_MODEL), lambda s: (jnp.minimum((s + 2) * blk_per_tile, last_blk), 0)),
            in_hbm,
            in_hbm,
            pl.BlockSpec(memory_space=pltpu.SMEM),
            resident((N_KV_HEADS, GROUP * BLOCK, 3 * BLOCK)),
            resident(inv_count.shape),
            in_hbm,
            in_hbm,
            in_hbm,
            in_hbm,
            in_hbm,
            in_hbm,
        ],
        out_specs=pl.BlockSpec((TILE, D_MODEL), lambda s: (s, 0)),
        out_shape=jax.ShapeDtypeStruct((batch * seq_len, d_model), x.dtype),
        scratch_shapes=[
            pltpu.VMEM((D_MODEL, IN_WIDTH), bf16),
            pltpu.VMEM((pool_rows, POOL_GROUP), bf16),
            pltpu.VMEM((ATTN_WIDTH, D_MODEL), bf16),
            pltpu.VMEM((POOL_WIDTH, D_MODEL), bf16),
            pltpu.VMEM((D_MODEL, D_MODEL), bf16),
            pltpu.VMEM((W_SLOTS, W_IN_CHUNK, IN_WIDTH), jnp.float32),
            pltpu.VMEM((W_SLOTS, W_SQ_CHUNK, D_MODEL), jnp.float32),
            pltpu.VMEM((W_SLOTS, W_SQ_CHUNK, POOL_GROUP), jnp.float32),
            pltpu.SemaphoreType.DMA((4, W_SLOTS)),
            pltpu.VMEM((VEC_ROWS, D_MODEL), jnp.float32),
            pltpu.VMEM((N_KV_HEADS, GROUP * BLOCK, 1), jnp.float32),
            pltpu.VMEM((ROWS, D_MODEL), bf16),
            pltpu.VMEM((N_KV_HEADS, ROWS, 2 * HEAD_DIM), bf16),
            pltpu.VMEM((N_KV_HEADS, ROWS, 2 * HEAD_DIM), bf16),
            pltpu.VMEM((TILE, ATTN_WIDTH), bf16),
            pltpu.VMEM((TILE, ATTN_WIDTH), jnp.float32),
        ],
        compiler_params=pltpu.CompilerParams(
            dimension_semantics=("arbitrary",),
            vmem_limit_bytes=VMEM_LIMIT_BYTES),
        name="hybrid_block",
    )

    x2d = x.reshape(batch * seq_len, d_model)
    y2d = call(x2d, x2d, x2d,
               norm_g, w_in[0], attn_sink[0].astype(jnp.float32), bias, inv_count,
               pool_w[0].reshape(pool_rows, POOL_GROUP), pool_scale,
               w_branch_a[0], w_branch_b[0], w_out[0], final_norm_g.reshape(1, D_MODEL))
    return y2d.reshape(batch, seq_len, d_model)
```

```python
import functools

import numpy as np
import jax
import jax.numpy as jnp
from jax import lax
from jax.experimental import pallas as pl
from jax.experimental.pallas import tpu as pltpu

D_MODEL = 1024
N_Q_HEADS = 8
N_KV_HEADS = 2
GROUP = N_Q_HEADS // N_KV_HEADS
HEAD_DIM = 64
ATTN_WIDTH = N_Q_HEADS * HEAD_DIM
KV_WIDTH = N_KV_HEADS * HEAD_DIM
WINDOW = 128
BLOCK = 128
POOL_WIDTH = 512
POOL_WINDOWS = (2, 4, 8, 16)
POOL_GROUP = 128
EPS = 1e-6
NEG_INF = -1e30
LOG2_E = 1.4426950408889634

OFF_Q = 0
OFF_K = OFF_Q + ATTN_WIDTH
OFF_V = OFF_K + KV_WIDTH
OFF_ZA = OFF_V + KV_WIDTH
OFF_U = OFF_ZA + ATTN_WIDTH
OFF_ZB = OFF_U + POOL_WIDTH
OFF_GA = OFF_ZB + POOL_WIDTH
OFF_GB = OFF_GA + D_MODEL
IN_WIDTH = OFF_GB + D_MODEL

TILE = 512
HALO = BLOCK
ROWS = TILE + 2 * HALO
MXU_COLS = 256
TAIL_ROWS = 256
SCORE_LEAD = 2
FILL_PER_STEP = 1
POOL_EDGE_ROWS = 16
POOL_LEAD = 16
POOL_TAIL = 32
W_IN_CHUNK = 64
W_SQ_CHUNK = 128
W_SLOTS = 3
VMEM_LIMIT_BYTES = 56 * 1024 * 1024
VEC_ROWS = 4
VEC_NORM_G, VEC_FINAL_G, VEC_POOL_SCALE = 0, 1, 2


def _attention_bias():
    r = np.arange(BLOCK)[:, None]
    c = np.arange(3 * BLOCK)[None, :]
    dist = np.abs(r - (c - BLOCK))
    in_window = dist <= WINDOW
    slopes = np.exp2(-8.0 * np.arange(1, N_Q_HEADS + 1, dtype=np.float64) / N_Q_HEADS)
    alibi = -slopes[:, None, None] * dist[None].astype(np.float64)
    out = np.where(in_window, alibi * LOG2_E, NEG_INF)
    return out.reshape(N_KV_HEADS, GROUP * BLOCK, 3 * BLOCK).astype(np.float32)


def _pool_inv_count(seq_len):
    out = np.zeros((2, 2, len(POOL_WINDOWS), POOL_EDGE_ROWS, 1), np.float64)
    for gi, w in enumerate(POOL_WINDOWS):
        half = w // 2
        out[:, 0, gi] = 1.0 / w
        for end, pos in enumerate([np.arange(POOL_EDGE_ROWS), seq_len - POOL_EDGE_ROWS + np.arange(POOL_EDGE_ROWS)]):
            count = np.minimum(pos + half, seq_len) - np.maximum(pos - half, 0)
            out[end, 1, gi, :, 0] = 1.0 / count
    return np.broadcast_to(out, out.shape[:-1] + (POOL_GROUP,)).astype(np.float32)


def _rms_norm(x, g):
    ms = jnp.mean(x * x, axis=-1, keepdims=True)
    return x * lax.rsqrt(ms + EPS) * g


def _sigmoid(x):
    return 0.5 * jnp.tanh(0.5 * x) + 0.5


def _silu(x):
    return x * _sigmoid(x)


def _cast_weights(streams):
    rings = []
    for pairs, stage_ref, sem_ref in streams:
        chunk = stage_ref.shape[1]
        jobs = [(src, dst, r) for src, dst in pairs for r in range(0, dst.shape[0], chunk)]
        rings.append((jobs, stage_ref, sem_ref))

    def copy(ring, c):
        jobs, stage_ref, sem_ref = ring
        src, _, r = jobs[c]
        slot = c % stage_ref.shape[0]
        return pltpu.make_async_copy(src.at[pl.ds(r, stage_ref.shape[1]), :], stage_ref.at[slot], sem_ref.at[slot])

    for ring in rings:
        for c in range(min(ring[1].shape[0], len(ring[0]))):
            copy(ring, c).start()
    for c in range(max(len(ring[0]) for ring in rings)):
        for ring in rings:
            jobs, stage_ref, _ = ring
            if c < len(jobs):
                n_slots, chunk = stage_ref.shape[0], stage_ref.shape[1]
                _, dst, r = jobs[c]
                copy(ring, c).wait()
                dst[r:r + chunk, :] = stage_ref[c % n_slots].astype(dst.dtype)
                if c + n_slots < len(jobs):
                    copy(ring, c + n_slots).start()


def _block_kernel(x_cur_ref, x_nxt_ref, x_after_ref, norm_g_hbm, w_in_hbm, sink_smem, bias_ref, cnt_ref,
                  pool_w_hbm, pool_scale_hbm, w_a_hbm, w_b_hbm, w_out_hbm, final_g_hbm,
                  out_ref, w_in_ref, pool_w_ref, w_a_ref, w_b_ref, w_out_ref, stage_in, stage_sq, stage_pool, sems,
                  vec_ref, sink_ref,
                  h_scr, k_scr, v_scr, q_scr, attn_scr, *, tiles_per_seq):
    f32, bf16 = jnp.float32, jnp.bfloat16
    step = pl.program_id(0)
    i = lax.rem(step, tiles_per_seq)
    n_blk = TILE // BLOCK
    is_first = i == 0
    is_last = i == tiles_per_seq - 1
    lane = lax.broadcasted_iota(jnp.int32, (1, 2 * HEAD_DIM), 1)
    low = lane < HEAD_DIM

    def proj(h, off, width):
        return jnp.dot(h, w_in_ref[:, off:off + width], preferred_element_type=f32)

    def proj_chunks(rows, off, width, act=lambda a: a):
        return [lambda c=c: act(proj(h_scr[rows, :], off + c, MXU_COLS)) for c in range(0, width, MXU_COLS)]

    def normed(x_ref, r):
        return _rms_norm(x_ref[r:r + BLOCK, :], vec_ref[VEC_NORM_G:VEC_NORM_G + 1, :]).astype(bf16)

    def store_kv(h_rows, dst):
        n = h_rows.shape[0]
        kv = proj(h_rows, OFF_K, 2 * KV_WIDTH)
        for scr, a in ((k_scr, kv[:, :KV_WIDTH]), (v_scr, kv[:, KV_WIDTH:])):
            swapped = pltpu.roll(a, HEAD_DIM, axis=1)
            scr[0, dst:dst + n, :] = jnp.where(low, a, swapped).astype(bf16)
            scr[1, dst:dst + n, :] = jnp.where(low, swapped, a).astype(bf16)

    def store_q(h_rows):
        q = (proj(h_rows, OFF_Q, ATTN_WIDTH) * (HEAD_DIM ** -0.5 * LOG2_E)).astype(bf16)
        for pair in range(N_Q_HEADS // 2):
            q_scr[pair] = q[:, pair * 2 * HEAD_DIM:(pair + 1) * 2 * HEAD_DIM]

    @pl.when(step == 0)
    def _prepare_first_tile():
        vec_copies = [
            pltpu.make_async_copy(norm_g_hbm, vec_ref.at[pl.ds(VEC_NORM_G, 1), :], sems.at[3, 0]),
            pltpu.make_async_copy(final_g_hbm, vec_ref.at[pl.ds(VEC_FINAL_G, 1), :], sems.at[3, 1]),
            pltpu.make_async_copy(pool_scale_hbm, vec_ref.at[pl.ds(VEC_POOL_SCALE, 1), pl.ds(0, POOL_WIDTH)],
                                  sems.at[3, 2]),
        ]
        for cp in vec_copies:
            cp.start()
        row = lax.broadcasted_iota(jnp.int32, (GROUP * BLOCK, 1), 0)
        for hk in range(N_KV_HEADS):
            col = jnp.zeros((GROUP * BLOCK, 1), f32)
            for g in range(GROUP):
                col = jnp.where((row >= g * BLOCK) & (row < (g + 1) * BLOCK), sink_smem[hk * GROUP + g] * LOG2_E, col)
            sink_ref[hk] = col
        for cp in vec_copies:
            cp.wait()
        _cast_weights([
            ([(w_in_hbm, w_in_ref)], stage_in, sems.at[0]),
            ([(w_out_hbm, w_out_ref), (w_a_hbm, w_a_ref), (w_b_hbm, w_b_ref)], stage_sq, sems.at[1]),
            ([(pool_w_hbm, pool_w_ref)], stage_pool, sems.at[2]),
        ])
        zeros = jnp.zeros((HALO, D_MODEL), bf16)
        h_new = jnp.concatenate([normed(x_cur_ref, r) for r in range(0, TILE, BLOCK)]
                                + [normed(x_nxt_ref, 0)], axis=0)
        h_scr[0:HALO, :] = zeros
        h_scr[HALO:ROWS, :] = h_new
        store_kv(zeros, 0)
        store_kv(h_new, HALO)
        store_q(h_new[0:TILE])

    zero = jnp.zeros((), bf16)
    key_col = lax.broadcasted_iota(jnp.int32, (1, 3 * BLOCK), 1)

    def scores(hk, j):
        rows = slice(j * BLOCK, (j + 1) * BLOCK)
        pieces = []
        for pair in range(GROUP // 2):
            qp = q_scr[hk * GROUP // 2 + pair, rows, :]
            pieces.append(jnp.where(low, qp, zero))
            pieces.append(jnp.where(low, zero, qp))
        lhs = jnp.concatenate(pieces, axis=0)
        kwin = k_scr[hk, j * BLOCK:j * BLOCK + 3 * BLOCK, :]
        s = lax.dot_general(lhs, kwin, (((1,), (1,)), ((), ())), preferred_element_type=f32)
        s = s + bias_ref[hk]
        if j == 0:
            s = s + jnp.where(is_first & (key_col < BLOCK), NEG_INF, 0.0)
        elif j == n_blk - 1:
            s = s + jnp.where(is_last & (key_col >= 2 * BLOCK), NEG_INF, 0.0)
        return s

    def softmax(s, hk):
        sink_col = sink_ref[hk]
        m = jnp.maximum(jnp.max(s, axis=-1, keepdims=True), sink_col)
        p = jnp.exp2(s - m)
        denom = jnp.sum(p, axis=-1, keepdims=True) + jnp.exp2(sink_col - m)
        return p.astype(bf16), denom

    def weighted_values(p, denom, hk, j):
        rows = slice(j * BLOCK, (j + 1) * BLOCK)
        vwin = v_scr[hk, j * BLOCK:j * BLOCK + 3 * BLOCK, :]
        o = jnp.dot(p, vwin, preferred_element_type=f32) / denom
        for pair in range(GROUP // 2):
            even = o[(2 * pair) * BLOCK:(2 * pair + 1) * BLOCK]
            odd = o[(2 * pair + 1) * BLOCK:(2 * pair + 2) * BLOCK]
            attn_scr[hk * GROUP // 2 + pair, rows, :] = jnp.where(low, even, odd)

    def pooled_group(u, gi):
        w = POOL_WINDOWS[gi]
        r0, n = POOL_LEAD, TILE
        if w == 2:
            wsum = u[r0 - 1:r0 - 1 + n] + u[r0:r0 + n]
        else:
            c = u[0:n + 32] + u[1:n + 33]
            if w == 4:
                wsum = c[r0 - 2:r0 - 2 + n] + c[r0:r0 + n]
            else:
                e = c[0:n + 24] + c[2:n + 26]
                if w == 8:
                    wsum = e[r0 - 4:r0 - 4 + n] + e[r0:r0 + n]
                else:
                    f = e[0:n + 16] + e[4:n + 20]
                    wsum = f[r0 - 8:r0 - 8 + n] + f[r0:r0 + n]
        edge = POOL_EDGE_ROWS
        head = wsum[:edge] * cnt_ref[0, jnp.where(is_first, 1, 0), gi]
        body = wsum[edge:n - edge] * (1.0 / w)
        tail = wsum[n - edge:] * cnt_ref[1, jnp.where(is_last, 1, 0), gi]
        pooled = jnp.concatenate([head, body, tail], axis=0) - u[r0:r0 + n]
        pool_w = pool_w_ref[gi * POOL_GROUP:(gi + 1) * POOL_GROUP, :]
        return jnp.dot(pooled.astype(bf16), pool_w, preferred_element_type=f32)

    centre = slice(HALO, HALO + TILE)
    pool_rows = slice(HALO - POOL_LEAD, HALO + TILE + POOL_TAIL)
    work = [(hk, j) for j in range(n_blk) for hk in range(N_KV_HEADS)]
    fillers = (proj_chunks(centre, OFF_GA, D_MODEL, _sigmoid)
               + proj_chunks(centre, OFF_GB, D_MODEL, _sigmoid)
               + proj_chunks(centre, OFF_ZA, ATTN_WIDTH)
               + proj_chunks(pool_rows, OFF_U, POOL_WIDTH)
               + proj_chunks(centre, OFF_ZB, POOL_WIDTH))
    filled = []
    s_list = [scores(*work[n]) for n in range(SCORE_LEAD)]
    p_prev = None
    for n, (hk, j) in enumerate(work):
        filled += [f() for f in fillers[FILL_PER_STEP * n:FILL_PER_STEP * (n + 1)]]
        if n + SCORE_LEAD < len(work):
            s_list.append(scores(*work[n + SCORE_LEAD]))
        p_cur = softmax(s_list[n], hk)
        if p_prev is not None:
            weighted_values(*p_prev, *work[n - 1])
        p_prev = p_cur
    filled += [f() for f in fillers[FILL_PER_STEP * len(work):]]
    weighted_values(*p_prev, *work[-1])

    def take(width):
        chunks = [filled.pop(0) for _ in range(width // MXU_COLS)]
        return jnp.concatenate(chunks, axis=1)

    gate_a = take(D_MODEL)
    gate_b = take(D_MODEL)
    z_a = take(ATTN_WIDTH)
    u_all = take(POOL_WIDTH)
    z_b = take(POOL_WIDTH)
    u_all = jnp.concatenate([jnp.where(is_first, 0.0, u_all[:POOL_LEAD]),
                             u_all[POOL_LEAD:POOL_LEAD + TILE],
                             jnp.where(is_last, 0.0, u_all[POOL_LEAD + TILE:])], axis=0)

    attn = jnp.concatenate([attn_scr[pair] for pair in range(N_Q_HEADS // 2)], axis=1)
    gated_a = (attn * _silu(z_a)).astype(bf16)
    y_a = jnp.dot(gated_a, w_a_ref[...], preferred_element_type=f32)
    mixed = [pooled_group(u_all[:, gi * POOL_GROUP:(gi + 1) * POOL_GROUP], gi)
             for gi in range(len(POOL_WINDOWS))]
    mixed = jnp.concatenate(mixed, axis=1) * vec_ref[VEC_POOL_SCALE:VEC_POOL_SCALE + 1, 0:POOL_WIDTH]
    gated_b = (mixed * _silu(z_b)).astype(bf16)
    y_b = jnp.dot(gated_b, w_b_ref[...], preferred_element_type=f32)
    t_a = gate_a * y_a

    final_g = vec_ref[VEC_FINAL_G:VEC_FINAL_G + 1, :]
    for r in range(0, TILE, TAIL_ROWS):
        rows = slice(r, r + TAIL_ROWS)
        merged = (t_a[rows] + gate_b[rows] * y_b[rows]).astype(bf16)
        y = x_cur_ref[rows, :] + jnp.dot(merged, w_out_ref[...], preferred_element_type=f32)
        out_ref[rows, :] = _rms_norm(y, final_g)

    h_new = jnp.concatenate([normed(x_nxt_ref, r) for r in range(HALO, TILE, BLOCK)]
                            + [normed(x_after_ref, 0)], axis=0)
    h_keep = h_scr[TILE:ROWS, :]
    k_keep, v_keep = k_scr[:, TILE:ROWS, :], v_scr[:, TILE:ROWS, :]
    h_scr[0:2 * HALO, :] = h_keep
    h_scr[2 * HALO:ROWS, :] = h_new
    k_scr[:, 0:2 * HALO, :] = k_keep
    v_scr[:, 0:2 * HALO, :] = v_keep
    for r in range(0, TILE, 2 * HALO):
        store_kv(h_new[r:r + 2 * HALO], 2 * HALO + r)
    store_q(jnp.concatenate([h_keep[HALO:], h_new[:TILE - HALO]], axis=0))


def kernel(x, norm_g, w_in, attn_sink, pool_w, pool_scale, w_branch_a, w_branch_b, w_out, final_norm_g):
    batch, seq_len, d_model = x.shape
    depth = norm_g.shape[0]
    assert depth == 1, "the final RMSNorm is fused into the single layer's kernel"
    assert d_model == D_MODEL and w_in.shape[-1] == IN_WIDTH
    assert seq_len % TILE == 0 and seq_len // BLOCK >= 2 and seq_len >= 2 * POOL_EDGE_ROWS
    assert POOL_LEAD >= max(POOL_WINDOWS) // 2 and POOL_EDGE_ROWS >= max(POOL_WINDOWS) // 2
    assert TILE // BLOCK >= 2 and TILE // TAIL_ROWS >= 2
    bf16 = jnp.bfloat16
    bias = jnp.asarray(_attention_bias())
    inv_count = jnp.asarray(_pool_inv_count(seq_len))
    tiles_per_seq = seq_len // TILE
    n_steps = batch * tiles_per_seq
    blk_per_tile = TILE // BLOCK
    last_blk = n_steps * blk_per_tile - 1
    pool_rows = len(POOL_WINDOWS) * POOL_GROUP

    def resident(shape):
        return pl.BlockSpec(shape, lambda s: (0,) * len(shape), pipeline_mode=pl.Buffered(1))

    in_hbm = pl.BlockSpec(memory_space=pl.ANY)
    call = pl.pallas_call(
        functools.partial(_block_kernel, tiles_per_seq=tiles_per_seq),
        grid=(n_steps,),
        in_specs=[
            pl.BlockSpec((TILE, D_MODEL), lambda s: (s, 0)),
            pl.BlockSpec((TILE, D_MODEL), lambda s: (jnp.minimum(s + 1, n_steps - 1), 0)),
            pl.BlockSpec((BLOCK, D_MODEL), lambda s: (jnp.minimum((s + 2) * blk_per_tile, last_blk), 0)),
            in_hbm,
            in_hbm,
            pl.BlockSpec(memory_space=pltpu.SMEM),
            resident((N_KV_HEADS, GROUP * BLOCK, 3 * BLOCK)),
            resident(inv_count.shape),
            in_hbm,
            in_hbm,
            in_hbm,
            in_hbm,
            in_hbm,
            in_hbm,
        ],
        out_specs=pl.BlockSpec((TILE, D_MODEL), lambda s: (s, 0)),
        out_shape=jax.ShapeDtypeStruct((batch * seq_len, d_model), x.dtype),
        scratch_shapes=[
            pltpu.VMEM((D_MODEL, IN_WIDTH), bf16),
            pltpu.VMEM((pool_rows, POOL_GROUP), bf16),
            pltpu.VMEM((ATTN_WIDTH, D_MODEL), bf16),
            pltpu.VMEM((POOL_WIDTH, D_MODEL), bf16),
            pltpu.VMEM((D_MODEL, D_MODEL), bf16),
            pltpu.VMEM((W_SLOTS, W_IN_CHUNK, IN_WIDTH), jnp.float32),
            pltpu.VMEM((W_SLOTS, W_SQ_CHUNK, D_MODEL), jnp.float32),
            pltpu.VMEM((W_SLOTS, W_SQ_CHUNK, POOL_GROUP), jnp.float32),
            pltpu.SemaphoreType.DMA((4, W_SLOTS)),
            pltpu.VMEM((VEC_ROWS, D_MODEL), jnp.float32),
            pltpu.VMEM((N_KV_HEADS, GROUP * BLOCK, 1), jnp.float32),
            pltpu.VMEM((ROWS, D_MODEL), bf16),
            pltpu.VMEM((N_KV_HEADS, ROWS, 2 * HEAD_DIM), bf16),
            pltpu.VMEM((N_KV_HEADS, ROWS, 2 * HEAD_DIM), bf16),
            pltpu.VMEM((N_Q_HEADS // 2, TILE, 2 * HEAD_DIM), bf16),
            pltpu.VMEM((N_Q_HEADS // 2, TILE, 2 * HEAD_DIM), jnp.float32),
        ],
        compiler_params=pltpu.CompilerParams(
            dimension_semantics=("arbitrary",),
            vmem_limit_bytes=VMEM_LIMIT_BYTES),
        name="hybrid_block",
    )

    x2d = x.reshape(batch * seq_len, d_model)
    y2d = call(x2d, x2d, x2d,
               norm_g, w_in[0], attn_sink[0].astype(jnp.float32), bias, inv_count,
               pool_w[0].reshape(pool_rows, POOL_GROUP), pool_scale,
               w_branch_a[0], w_branch_b[0], w_out[0], final_norm_g.reshape(1, D_MODEL))
    return y2d.reshape(batch, seq_len, d_model)
```

```python
import functools

import numpy as np
import jax
import jax.numpy as jnp
from jax import lax
from jax.experimental import pallas as pl
from jax.experimental.pallas import tpu as pltpu

D_MODEL = 1024
N_Q_HEADS = 8
N_KV_HEADS = 2
GROUP = N_Q_HEADS // N_KV_HEADS
HEAD_DIM = 64
ATTN_WIDTH = N_Q_HEADS * HEAD_DIM
KV_WIDTH = N_KV_HEADS * HEAD_DIM
WINDOW = 128
BLOCK = 128
POOL_WIDTH = 512
POOL_WINDOWS = (2, 4, 8, 16)
POOL_GROUP = 128
EPS = 1e-6
NEG_INF = -1e30
LOG2_E = 1.4426950408889634

OFF_Q = 0
OFF_K = OFF_Q + ATTN_WIDTH
OFF_V = OFF_K + KV_WIDTH
OFF_ZA = OFF_V + KV_WIDTH
OFF_U = OFF_ZA + ATTN_WIDTH
OFF_ZB = OFF_U + POOL_WIDTH
OFF_GA = OFF_ZB + POOL_WIDTH
OFF_GB = OFF_GA + D_MODEL
IN_WIDTH = OFF_GB + D_MODEL

TILE = 512
HALO = BLOCK
ROWS = TILE + 2 * HALO
MXU_COLS = 256
TAIL_ROWS = 256
SCORE_LEAD = 2
FILL_PER_STEP = 1
POOL_EDGE_ROWS = 16
POOL_LEAD = 16
POOL_TAIL = 32
W_IN_CHUNK = 64
W_SQ_CHUNK = 128
W_SLOTS = 3
VMEM_LIMIT_BYTES = 56 * 1024 * 1024
VEC_ROWS = 4
VEC_NORM_G, VEC_FINAL_G, VEC_POOL_SCALE = 0, 1, 2


def _attention_bias():
    r = np.arange(BLOCK)[:, None]
    c = np.arange(3 * BLOCK)[None, :]
    dist = np.abs(r - (c - BLOCK))
    in_window = dist <= WINDOW
    slopes = np.exp2(-8.0 * np.arange(1, N_Q_HEADS + 1, dtype=np.float64) / N_Q_HEADS)
    alibi = -slopes[:, None, None] * dist[None].astype(np.float64)
    out = np.where(in_window, alibi * LOG2_E, NEG_INF)
    return out.reshape(N_KV_HEADS, GROUP * BLOCK, 3 * BLOCK).astype(np.float32)


def _pool_inv_count(seq_len):
    out = np.zeros((2, 2, len(POOL_WINDOWS), POOL_EDGE_ROWS, 1), np.float64)
    for gi, w in enumerate(POOL_WINDOWS):
        half = w // 2
        out[:, 0, gi] = 1.0 / w
        for end, pos in enumerate([np.arange(POOL_EDGE_ROWS), seq_len - POOL_EDGE_ROWS + np.arange(POOL_EDGE_ROWS)]):
            count = np.minimum(pos + half, seq_len) - np.maximum(pos - half, 0)
            out[end, 1, gi, :, 0] = 1.0 / count
    return np.broadcast_to(out, out.shape[:-1] + (POOL_GROUP,)).astype(np.float32)


def _rms_norm(x, g):
    ms = jnp.mean(x * x, axis=-1, keepdims=True)
    return x * lax.rsqrt(ms + EPS) * g


def _sigmoid(x):
    return 0.5 * jnp.tanh(0.5 * x) + 0.5


def _silu(x):
    return x * _sigmoid(x)


def _cast_weights(streams):
    rings = []
    for pairs, stage_ref, sem_ref in streams:
        chunk = stage_ref.shape[1]
        jobs = [(src, dst, r) for src, dst in pairs for r in range(0, dst.shape[0], chunk)]
        rings.append((jobs, stage_ref, sem_ref))

    def copy(ring, c):
        jobs, stage_ref, sem_ref = ring
        src, _, r = jobs[c]
        slot = c % stage_ref.shape[0]
        return pltpu.make_async_copy(src.at[pl.ds(r, stage_ref.shape[1]), :], stage_ref.at[slot], sem_ref.at[slot])

    for ring in rings:
        for c in range(min(ring[1].shape[0], len(ring[0]))):
            copy(ring, c).start()
    for c in range(max(len(ring[0]) for ring in rings)):
        for ring in rings:
            jobs, stage_ref, _ = ring
            if c < len(jobs):
                n_slots, chunk = stage_ref.shape[0], stage_ref.shape[1]
                _, dst, r = jobs[c]
                copy(ring, c).wait()
                dst[r:r + chunk, :] = stage_ref[c % n_slots].astype(dst.dtype)
                if c + n_slots < len(jobs):
                    copy(ring, c + n_slots).start()


def _block_kernel(x_cur_ref, x_nxt_ref, x_after_ref, norm_g_hbm, w_in_hbm, sink_smem, bias_ref, cnt_ref,
                  pool_w_hbm, pool_scale_hbm, w_a_hbm, w_b_hbm, w_out_hbm, final_g_hbm,
                  out_ref, w_in_ref, pool_w_ref, w_a_ref, w_b_ref, w_out_ref, stage_in, stage_sq, stage_pool, sems,
                  vec_ref, sink_ref, u_keep,
                  h_scr, k_scr, v_scr, q_scr, attn_scr, *, tiles_per_seq):
    f32, bf16 = jnp.float32, jnp.bfloat16
    step = pl.program_id(0)
    i = lax.rem(step, tiles_per_seq)
    n_blk = TILE // BLOCK
    is_first = i == 0
    is_last = i == tiles_per_seq - 1
    lane = lax.broadcasted_iota(jnp.int32, (1, 2 * HEAD_DIM), 1)
    low = lane < HEAD_DIM

    def proj(h, off, width):
        return jnp.dot(h, w_in_ref[:, off:off + width], preferred_element_type=f32)

    def proj_chunks(rows, off, width, act=lambda a: a):
        return [lambda c=c: act(proj(h_scr[rows, :], off + c, MXU_COLS)) for c in range(0, width, MXU_COLS)]

    def normed(x_ref, r):
        return _rms_norm(x_ref[r:r + BLOCK, :], vec_ref[VEC_NORM_G:VEC_NORM_G + 1, :]).astype(bf16)

    def store_kv(h_rows, dst):
        n = h_rows.shape[0]
        kv = proj(h_rows, OFF_K, 2 * KV_WIDTH)
        for scr, a in ((k_scr, kv[:, :KV_WIDTH]), (v_scr, kv[:, KV_WIDTH:])):
            swapped = pltpu.roll(a, HEAD_DIM, axis=1)
            scr[0, dst:dst + n, :] = jnp.where(low, a, swapped).astype(bf16)
            scr[1, dst:dst + n, :] = jnp.where(low, swapped, a).astype(bf16)

    def store_q(h_rows):
        q = (proj(h_rows, OFF_Q, ATTN_WIDTH) * (HEAD_DIM ** -0.5 * LOG2_E)).astype(bf16)
        for pair in range(N_Q_HEADS // 2):
            q_scr[pair] = q[:, pair * 2 * HEAD_DIM:(pair + 1) * 2 * HEAD_DIM]

    @pl.when(step == 0)
    def _prepare_first_tile():
        vec_copies = [
            pltpu.make_async_copy(norm_g_hbm, vec_ref.at[pl.ds(VEC_NORM_G, 1), :], sems.at[3, 0]),
            pltpu.make_async_copy(final_g_hbm, vec_ref.at[pl.ds(VEC_FINAL_G, 1), :], sems.at[3, 1]),
            pltpu.make_async_copy(pool_scale_hbm, vec_ref.at[pl.ds(VEC_POOL_SCALE, 1), pl.ds(0, POOL_WIDTH)],
                                  sems.at[3, 2]),
        ]
        for cp in vec_copies:
            cp.start()
        row = lax.broadcasted_iota(jnp.int32, (GROUP * BLOCK, 1), 0)
        for hk in range(N_KV_HEADS):
            col = jnp.zeros((GROUP * BLOCK, 1), f32)
            for g in range(GROUP):
                col = jnp.where((row >= g * BLOCK) & (row < (g + 1) * BLOCK), sink_smem[hk * GROUP + g] * LOG2_E, col)
            sink_ref[hk] = col
        for cp in vec_copies:
            cp.wait()
        _cast_weights([
            ([(w_in_hbm, w_in_ref)], stage_in, sems.at[0]),
            ([(w_out_hbm, w_out_ref), (w_a_hbm, w_a_ref), (w_b_hbm, w_b_ref)], stage_sq, sems.at[1]),
            ([(pool_w_hbm, pool_w_ref)], stage_pool, sems.at[2]),
        ])
        zeros = jnp.zeros((HALO, D_MODEL), bf16)
        h_new = jnp.concatenate([normed(x_cur_ref, r) for r in range(0, TILE, BLOCK)]
                                + [normed(x_nxt_ref, 0)], axis=0)
        h_scr[0:HALO, :] = zeros
        h_scr[HALO:ROWS, :] = h_new
        store_kv(zeros, 0)
        store_kv(h_new, HALO)
        store_q(h_new[0:TILE])
        u_keep[...] = proj(h_scr[HALO - POOL_LEAD:HALO + POOL_TAIL, :], OFF_U, POOL_WIDTH)

    zero = jnp.zeros((), bf16)
    key_col = lax.broadcasted_iota(jnp.int32, (1, 3 * BLOCK), 1)

    def scores(hk, j):
        rows = slice(j * BLOCK, (j + 1) * BLOCK)
        pieces = []
        for pair in range(GROUP // 2):
            qp = q_scr[hk * GROUP // 2 + pair, rows, :]
            pieces.append(jnp.where(low, qp, zero))
            pieces.append(jnp.where(low, zero, qp))
        lhs = jnp.concatenate(pieces, axis=0)
        kwin = k_scr[hk, j * BLOCK:j * BLOCK + 3 * BLOCK, :]
        s = lax.dot_general(lhs, kwin, (((1,), (1,)), ((), ())), preferred_element_type=f32)
        s = s + bias_ref[hk]
        if j == 0:
            s = s + jnp.where(is_first & (key_col < BLOCK), NEG_INF, 0.0)
        elif j == n_blk - 1:
            s = s + jnp.where(is_last & (key_col >= 2 * BLOCK), NEG_INF, 0.0)
        return s

    def softmax(s, hk):
        sink_col = sink_ref[hk]
        m = jnp.maximum(jnp.max(s, axis=-1, keepdims=True), sink_col)
        p = jnp.exp2(s - m)
        denom = jnp.sum(p, axis=-1, keepdims=True) + jnp.exp2(sink_col - m)
        return p.astype(bf16), denom

    def weighted_values(p, denom, hk, j):
        rows = slice(j * BLOCK, (j + 1) * BLOCK)
        vwin = v_scr[hk, j * BLOCK:j * BLOCK + 3 * BLOCK, :]
        o = jnp.dot(p, vwin, preferred_element_type=f32) / denom
        for pair in range(GROUP // 2):
            even = o[(2 * pair) * BLOCK:(2 * pair + 1) * BLOCK]
            odd = o[(2 * pair + 1) * BLOCK:(2 * pair + 2) * BLOCK]
            attn_scr[hk * GROUP // 2 + pair, rows, :] = jnp.where(low, even, odd)

    def pooled_group(u, gi):
        w = POOL_WINDOWS[gi]
        r0, n = POOL_LEAD, TILE
        if w == 2:
            wsum = u[r0 - 1:r0 - 1 + n] + u[r0:r0 + n]
        else:
            c = u[0:n + 32] + u[1:n + 33]
            if w == 4:
                wsum = c[r0 - 2:r0 - 2 + n] + c[r0:r0 + n]
            else:
                e = c[0:n + 24] + c[2:n + 26]
                if w == 8:
                    wsum = e[r0 - 4:r0 - 4 + n] + e[r0:r0 + n]
                else:
                    f = e[0:n + 16] + e[4:n + 20]
                    wsum = f[r0 - 8:r0 - 8 + n] + f[r0:r0 + n]
        edge = POOL_EDGE_ROWS
        head = wsum[:edge] * cnt_ref[0, jnp.where(is_first, 1, 0), gi]
        body = wsum[edge:n - edge] * (1.0 / w)
        tail = wsum[n - edge:] * cnt_ref[1, jnp.where(is_last, 1, 0), gi]
        pooled = jnp.concatenate([head, body, tail], axis=0) - u[r0:r0 + n]
        pool_w = pool_w_ref[gi * POOL_GROUP:(gi + 1) * POOL_GROUP, :]
        return jnp.dot(pooled.astype(bf16), pool_w, preferred_element_type=f32)

    centre = slice(HALO, HALO + TILE)
    pool_rows = slice(HALO + POOL_TAIL, HALO + POOL_TAIL + TILE)
    work = [(hk, j) for j in range(n_blk) for hk in range(N_KV_HEADS)]
    fillers = (proj_chunks(centre, OFF_GA, D_MODEL, _sigmoid)
               + proj_chunks(centre, OFF_GB, D_MODEL, _sigmoid)
               + proj_chunks(centre, OFF_ZA, ATTN_WIDTH)
               + proj_chunks(pool_rows, OFF_U, POOL_WIDTH)
               + proj_chunks(centre, OFF_ZB, POOL_WIDTH))
    filled = []
    s_list = [scores(*work[n]) for n in range(SCORE_LEAD)]
    p_prev = None
    for n, (hk, j) in enumerate(work):
        filled += [f() for f in fillers[FILL_PER_STEP * n:FILL_PER_STEP * (n + 1)]]
        if n + SCORE_LEAD < len(work):
            s_list.append(scores(*work[n + SCORE_LEAD]))
        p_cur = softmax(s_list[n], hk)
        if p_prev is not None:
            weighted_values(*p_prev, *work[n - 1])
        p_prev = p_cur
    filled += [f() for f in fillers[FILL_PER_STEP * len(work):]]
    weighted_values(*p_prev, *work[-1])

    def take(width):
        chunks = [filled.pop(0) for _ in range(width // MXU_COLS)]
        return jnp.concatenate(chunks, axis=1)

    gate_a = take(D_MODEL)
    gate_b = take(D_MODEL)
    z_a = take(ATTN_WIDTH)
    u_all = jnp.concatenate([u_keep[...], take(POOL_WIDTH)], axis=0)
    u_keep[...] = u_all[TILE:]
    z_b = take(POOL_WIDTH)
    u_all = jnp.concatenate([jnp.where(is_first, 0.0, u_all[:POOL_LEAD]),
                             u_all[POOL_LEAD:POOL_LEAD + TILE],
                             jnp.where(is_last, 0.0, u_all[POOL_LEAD + TILE:])], axis=0)

    attn = jnp.concatenate([attn_scr[pair] for pair in range(N_Q_HEADS // 2)], axis=1)
    gated_a = (attn * _silu(z_a)).astype(bf16)
    y_a = jnp.dot(gated_a, w_a_ref[...], preferred_element_type=f32)
    mixed = [pooled_group(u_all[:, gi * POOL_GROUP:(gi + 1) * POOL_GROUP], gi)
             for gi in range(len(POOL_WINDOWS))]
    mixed = jnp.concatenate(mixed, axis=1) * vec_ref[VEC_POOL_SCALE:VEC_POOL_SCALE + 1, 0:POOL_WIDTH]
    gated_b = (mixed * _silu(z_b)).astype(bf16)
    y_b = jnp.dot(gated_b, w_b_ref[...], preferred_element_type=f32)
    t_a = gate_a * y_a

    final_g = vec_ref[VEC_FINAL_G:VEC_FINAL_G + 1, :]
    for r in range(0, TILE, TAIL_ROWS):
        rows = slice(r, r + TAIL_ROWS)
        merged = (t_a[rows] + gate_b[rows] * y_b[rows]).astype(bf16)
        y = x_cur_ref[rows, :] + jnp.dot(merged, w_out_ref[...], preferred_element_type=f32)
        out_ref[rows, :] = _rms_norm(y, final_g)

    h_new = jnp.concatenate([normed(x_nxt_ref, r) for r in range(HALO, TILE, BLOCK)]
                            + [normed(x_after_ref, 0)], axis=0)
    h_keep = h_scr[TILE:ROWS, :]
    k_keep, v_keep = k_scr[:, TILE:ROWS, :], v_scr[:, TILE:ROWS, :]
    h_scr[0:2 * HALO, :] = h_keep
    h_scr[2 * HALO:ROWS, :] = h_new
    k_scr[:, 0:2 * HALO, :] = k_keep
    v_scr[:, 0:2 * HALO, :] = v_keep
    for r in range(0, TILE, 2 * HALO):
        store_kv(h_new[r:r + 2 * HALO], 2 * HALO + r)
    store_q(jnp.concatenate([h_keep[HALO:], h_new[:TILE - HALO]], axis=0))


def kernel(x, norm_g, w_in, attn_sink, pool_w, pool_scale, w_branch_a, w_branch_b, w_out, final_norm_g):
    batch, seq_len, d_model = x.shape
    depth = norm_g.shape[0]
    assert depth == 1, "the final RMSNorm is fused into the single layer's kernel"
    assert d_model == D_MODEL and w_in.shape[-1] == IN_WIDTH
    assert seq_len % TILE == 0 and seq_len // BLOCK >= 2 and seq_len >= 2 * POOL_EDGE_ROWS
    assert POOL_LEAD >= max(POOL_WINDOWS) // 2 and POOL_EDGE_ROWS >= max(POOL_WINDOWS) // 2
    assert TILE // BLOCK >= 2 and TILE // TAIL_ROWS >= 2
    bf16 = jnp.bfloat16
    bias = jnp.asarray(_attention_bias())
    inv_count = jnp.asarray(_pool_inv_count(seq_len))
    tiles_per_seq = seq_len // TILE
    n_steps = batch * tiles_per_seq
    blk_per_tile = TILE // BLOCK
    last_blk = n_steps * blk_per_tile - 1
    pool_rows = len(POOL_WINDOWS) * POOL_GROUP

    def resident(shape):
        return pl.BlockSpec(shape, lambda s: (0,) * len(shape), pipeline_mode=pl.Buffered(1))

    in_hbm = pl.BlockSpec(memory_space=pl.ANY)
    call = pl.pallas_call(
        functools.partial(_block_kernel, tiles_per_seq=tiles_per_seq),
        grid=(n_steps,),
        in_specs=[
            pl.BlockSpec((TILE, D_MODEL), lambda s: (s, 0)),
            pl.BlockSpec((TILE, D_MODEL), lambda s: (jnp.minimum(s + 1, n_steps - 1), 0)),
            pl.BlockSpec((BLOCK, D_MODEL), lambda s: (jnp.minimum((s + 2) * blk_per_tile, last_blk), 0)),
            in_hbm,
            in_hbm,
            pl.BlockSpec(memory_space=pltpu.SMEM),
            resident((N_KV_HEADS, GROUP * BLOCK, 3 * BLOCK)),
            resident(inv_count.shape),
            in_hbm,
            in_hbm,
            in_hbm,
            in_hbm,
            in_hbm,
            in_hbm,
        ],
        out_specs=pl.BlockSpec((TILE, D_MODEL), lambda s: (s, 0)),
        out_shape=jax.ShapeDtypeStruct((batch * seq_len, d_model), x.dtype),
        scratch_shapes=[
            pltpu.VMEM((D_MODEL, IN_WIDTH), bf16),
            pltpu.VMEM((pool_rows, POOL_GROUP), bf16),
            pltpu.VMEM((ATTN_WIDTH, D_MODEL), bf16),
            pltpu.VMEM((POOL_WIDTH, D_MODEL), bf16),
            pltpu.VMEM((D_MODEL, D_MODEL), bf16),
            pltpu.VMEM((W_SLOTS, W_IN_CHUNK, IN_WIDTH), jnp.float32),
            pltpu.VMEM((W_SLOTS, W_SQ_CHUNK, D_MODEL), jnp.float32),
            pltpu.VMEM((W_SLOTS, W_SQ_CHUNK, POOL_GROUP), jnp.float32),
            pltpu.SemaphoreType.DMA((4, W_SLOTS)),
            pltpu.VMEM((VEC_ROWS, D_MODEL), jnp.float32),
            pltpu.VMEM((N_KV_HEADS, GROUP * BLOCK, 1), jnp.float32),
            pltpu.VMEM((POOL_LEAD + POOL_TAIL, POOL_WIDTH), jnp.float32),
            pltpu.VMEM((ROWS, D_MODEL), bf16),
            pltpu.VMEM((N_KV_HEADS, ROWS, 2 * HEAD_DIM), bf16),
            pltpu.VMEM((N_KV_HEADS, ROWS, 2 * HEAD_DIM), bf16),
            pltpu.VMEM((N_Q_HEADS // 2, TILE, 2 * HEAD_DIM), bf16),
            pltpu.VMEM((N_Q_HEADS // 2, TILE, 2 * HEAD_DIM), jnp.float32),
        ],
        compiler_params=pltpu.CompilerParams(
            dimension_semantics=("arbitrary",),
            vmem_limit_bytes=VMEM_LIMIT_BYTES),
        name="hybrid_block",
    )

    x2d = x.reshape(batch * seq_len, d_model)
    y2d = call(x2d, x2d, x2d,
               norm_g, w_in[0], attn_sink[0].astype(jnp.float32), bias, inv_count,
               pool_w[0].reshape(pool_rows, POOL_GROUP), pool_scale,
               w_branch_a[0], w_branch_b[0], w_out[0], final_norm_g.reshape(1, D_MODEL))
    return y2d.reshape(batch, seq_len, d_model)
```
